```python
import math
import jax, jax.numpy as jnp
from jax import lax
import numpy as np

D_MODEL = 1024
BATCH = 4
SEQ = 8192
DEPTH = 2
DEC_BATCH = 128
DEC_SEQ = 4
PAST_LEN = 16384
PAGE_SIZE = 128

GLA_HEADS = 4
GLA_DK = 128
GLA_DV = 128
GLA_LOWRANK = 16
GLA_TAU = 16.0
GLA_CHUNK = 64
SSD_HEADS = 8
SSD_HEADDIM = 64
SSD_GROUPS = 2
SSD_HPG = SSD_HEADS // SSD_GROUPS
SSD_STATE = 128
SSD_CONV = 4
SSD_CHUNK = 64
SSD_INNER = SSD_HEADS * SSD_HEADDIM
SSD_CONV_DIM = SSD_INNER + 2 * SSD_GROUPS * SSD_STATE
SWA_HEADS = 8
SWA_KV_HEADS = 2
SWA_REP = SWA_HEADS // SWA_KV_HEADS
SWA_HEAD_DIM = 64
WINDOW = 128
MEM_LEN = 256
MEM_HEADS = 4
MEM_HEAD_DIM = D_MODEL // MEM_HEADS
N_GROUPS = 4
EXPERTS_PER_GROUP = 4
N_EXPERTS = N_GROUPS * EXPERTS_PER_GROUP
EXPERT_FF = 256
TOP_K_INNER = 2
N_BRANCHES = 3
EPS = 1e-6

IN_SIZES = (GLA_HEADS * GLA_DK, GLA_HEADS * GLA_DK, GLA_HEADS * GLA_DV, GLA_HEADS * GLA_DV, GLA_LOWRANK,
            SSD_INNER, SSD_CONV_DIM, SSD_HEADS,
            SWA_HEADS * SWA_HEAD_DIM, SWA_KV_HEADS * SWA_HEAD_DIM, SWA_KV_HEADS * SWA_HEAD_DIM,
            N_BRANCHES * D_MODEL)
IN_DIM = sum(IN_SIZES)

kernel_name = 'hybrid_gla_ssd_swa_mem_hmoe_step'


def rmsnorm(x, g):
    xf = x.astype(jnp.float32)
    y = xf * lax.rsqrt(jnp.mean(xf * xf, axis=-1, keepdims=True) + EPS)
    return (y * g.astype(jnp.float32)).astype(x.dtype)


def split_cols(h):
    parts, start = [], 0
    for n in IN_SIZES:
        parts.append(h[..., start:start + n])
        start += n
    return parts


def to_chunks(t, c):
    b, l = t.shape[:2]
    n = -(-l // c)
    t = jnp.pad(t, [(0, 0), (0, n * c - l)] + [(0, 0)] * (t.ndim - 2))
    return jnp.moveaxis(t.reshape((b, n, c) + t.shape[2:]), 1, 0).astype(jnp.float32)


def from_chunks(t, l):
    n, b, c = t.shape[:3]
    return jnp.moveaxis(t, 0, 1).reshape((b, n * c) + t.shape[3:])[:, :l]


def gla_chunked(q, k, v, log_a, s0):
    l = q.shape[1]
    c = min(GLA_CHUNK, l)
    causal = jnp.tril(jnp.ones((c, c), dtype=bool))[None, :, :, None, None]

    def step(s, inp):
        qi, ki, vi, ai = inp
        bcum = jnp.cumsum(ai, axis=1)
        decay = jnp.exp(jnp.where(causal, bcum[:, :, None] - bcum[:, None, :], -jnp.inf))
        scores = jnp.einsum('bthk,btshk,bshk->bhts', qi, decay, ki)
        o = (jnp.einsum('bhts,bshv->bthv', scores, vi)
             + jnp.einsum('bthk,bhkv->bthv', qi * jnp.exp(bcum), s))
        b_end = bcum[:, -1]
        s = (jnp.exp(b_end)[..., None] * s
             + jnp.einsum('bshk,bshv->bhkv', ki * jnp.exp(b_end[:, None] - bcum), vi))
        return s, o

    s, o = lax.scan(step, s0.astype(jnp.float32), tuple(to_chunks(t, c) for t in (q, k, v, log_a)))
    return from_chunks(o, l), s


def ssd_chunked(x, dt, a, bm, cm, h0):
    l = x.shape[1]
    c = min(SSD_CHUNK, l)
    causal = jnp.tril(jnp.ones((c, c), dtype=bool))[None, :, :, None, None]

    def step(h, inp):
        xi, dti, bi, ci = inp
        cum = jnp.cumsum(dti * a, axis=1)
        lmat = jnp.exp(jnp.where(causal, cum[:, :, None] - cum[:, None, :], -jnp.inf))
        w = jnp.einsum('btgn,bsgn->btsg', ci, bi)[..., None] * lmat * dti[:, None]
        y = (jnp.einsum('btsgr,bsgrp->btgrp', w, xi)
             + jnp.einsum('btgn,bgrpn->btgrp', ci, h) * jnp.exp(cum)[..., None])
        c_end = cum[:, -1]
        h = (jnp.exp(c_end)[..., None, None] * h
             + jnp.einsum('bsgr,bsgn,bsgrp->bgrpn', jnp.exp(c_end[:, None] - cum) * dti, bi, xi))
        return h, y

    h, y = lax.scan(step, h0.astype(jnp.float32), tuple(to_chunks(t, c) for t in (x, dt, bm, cm)))
    return from_chunks(y, l), h


def causal_conv(xbc, buf, conv_w, conv_b):
    l = xbc.shape[1]
    xp = jnp.concatenate([buf.astype(xbc.dtype), xbc], axis=1)
    y = conv_b
    for j in range(SSD_CONV):
        y = y + xp[:, j:j + l] * conv_w[j]
    return jax.nn.silu(y), xp[:, -(SSD_CONV - 1):]


def band_attn(q, k, v, qpos, kpos, sinks):
    s = jnp.einsum('bnqgrd,bnkgd->bngrqk', q, k).astype(jnp.float32) * SWA_HEAD_DIM ** -0.5
    rel = qpos[:, :, None] - kpos[:, None, :]
    mask = (rel >= 0) & (rel < WINDOW) & (kpos[:, None, :] >= 0)
    s = jnp.where(mask[None, :, None, None], s, -jnp.inf)
    sink = sinks.astype(jnp.float32).reshape(SWA_KV_HEADS, SWA_REP)[None, None, :, :, None, None]
    m = jnp.maximum(jnp.max(s, axis=-1, keepdims=True), sink)
    p = jnp.exp(s - m)
    p = p / (jnp.sum(p, axis=-1, keepdims=True) + jnp.exp(sink - m))
    return jnp.einsum('bngrqk,bnkgd->bnqgrd', p.astype(v.dtype), v)


def mixer(xn, lp, gla_s0, ssd_h0, conv_buf, swa_kbuf, swa_vbuf):
    b, l, _ = xn.shape
    gq, gk, gv, gr, ga, sz, sxbc, sdt, aq, ak, av, gate_logits = split_cols(xn @ lp['w_in'])
    q = gq.reshape(b, l, GLA_HEADS, GLA_DK) * GLA_DK ** -0.5
    k = gk.reshape(b, l, GLA_HEADS, GLA_DK)
    v = gv.reshape(b, l, GLA_HEADS, GLA_DV)
    log_a = jax.nn.log_sigmoid((ga @ lp['gla_w_a2']).astype(jnp.float32) + lp['gla_b_a']) / GLA_TAU
    o_gla, gla_s = gla_chunked(q, k, v, log_a.reshape(b, l, GLA_HEADS, GLA_DK), gla_s0)
    o_gla = rmsnorm(o_gla, lp['gla_norm']) * jax.nn.silu(gr.astype(jnp.float32)).reshape(b, l, GLA_HEADS, GLA_DV)
    o_gla = o_gla.reshape(b, l, GLA_HEADS * GLA_DV)
    xbc, conv_s = causal_conv(sxbc, conv_buf, lp['ssd_conv_w'], lp['ssd_conv_b'])
    xbc = xbc.astype(jnp.float32)
    xs = xbc[..., :SSD_INNER].reshape(b, l, SSD_GROUPS, SSD_HPG, SSD_HEADDIM)
    bm = xbc[..., SSD_INNER:SSD_INNER + SSD_GROUPS * SSD_STATE].reshape(b, l, SSD_GROUPS, SSD_STATE)
    cm = xbc[..., SSD_INNER + SSD_GROUPS * SSD_STATE:].reshape(b, l, SSD_GROUPS, SSD_STATE)
    dt = jax.nn.softplus(sdt.astype(jnp.float32) + lp['ssd_dt_bias']).reshape(b, l, SSD_GROUPS, SSD_HPG)
    a = -jnp.exp(lp['ssd_a_log'].astype(jnp.float32)).reshape(SSD_GROUPS, SSD_HPG)
    y, ssd_h = ssd_chunked(xs, dt, a, bm, cm, ssd_h0.reshape(b, SSD_GROUPS, SSD_HPG, SSD_HEADDIM, SSD_STATE))
    y = y + lp['ssd_d'].astype(jnp.float32).reshape(SSD_GROUPS, SSD_HPG)[..., None] * xs
    y = y.reshape(b, l, SSD_INNER) * jax.nn.silu(sz.astype(jnp.float32))
    gw = SSD_INNER // SSD_GROUPS
    y_ssd = rmsnorm(y.reshape(b, l, SSD_GROUPS, gw), lp['ssd_norm'].reshape(SSD_GROUPS, gw)).reshape(b, l, SSD_INNER)
    q = aq.reshape(b, l, SWA_KV_HEADS, SWA_REP, SWA_HEAD_DIM)
    k = ak.reshape(b, l, SWA_KV_HEADS, SWA_HEAD_DIM)
    v = av.reshape(b, l, SWA_KV_HEADS, SWA_HEAD_DIM)
    if swa_kbuf is None:
        nb = l // WINDOW
        kb = k.reshape(b, nb, WINDOW, SWA_KV_HEADS, SWA_HEAD_DIM)
        vb = v.reshape(b, nb, WINDOW, SWA_KV_HEADS, SWA_HEAD_DIM)
        prev = lambda t: jnp.pad(t, ((0, 0), (1, 0), (0, 0), (0, 0), (0, 0)))[:, :-1]
        pos = jnp.arange(l).reshape(nb, WINDOW)
        o_swa = band_attn(q.reshape(b, nb, WINDOW, SWA_KV_HEADS, SWA_REP, SWA_HEAD_DIM),
                          jnp.concatenate([prev(kb), kb], axis=2), jnp.concatenate([prev(vb), vb], axis=2),
                          pos, jnp.concatenate([pos - WINDOW, pos], axis=1), lp['swa_sinks'])
        nkeep = min(WINDOW, l)
        swa_k, swa_v = k[:, l - nkeep:], v[:, l - nkeep:]
    else:
        nbuf = swa_kbuf.shape[1]
        kall = jnp.concatenate([swa_kbuf.astype(k.dtype), k], axis=1)
        vall = jnp.concatenate([swa_vbuf.astype(v.dtype), v], axis=1)
        qpos = (PAST_LEN + jnp.arange(l))[None]
        kpos = (PAST_LEN - nbuf + jnp.arange(nbuf + l))[None]
        o_swa = band_attn(q[:, None], kall[:, None], vall[:, None], qpos, kpos, lp['swa_sinks'])
        swa_k, swa_v = kall[:, l:], vall[:, l:]
    o_swa = o_swa.reshape(b, l, SWA_HEADS * SWA_HEAD_DIM)
    g = jax.nn.sigmoid(gate_logits.astype(jnp.float32)).reshape(b, l, N_BRANCHES, D_MODEL)
    merged = (g[:, :, 0] * (o_gla @ lp['w_br_gla'])
              + g[:, :, 1] * (y_ssd @ lp['w_br_ssd'])
              + g[:, :, 2] * (o_swa @ lp['w_br_swa']))
    out = merged.astype(xn.dtype) @ lp['w_out']
    ssd_h = ssd_h.reshape(b, SSD_HEADS, SSD_HEADDIM, SSD_STATE)
    return out.astype(xn.dtype), gla_s, ssd_h, conv_s, swa_k, swa_v


def mem_attn(xn, mk, mv, wq, wo):
    b, l, d = xn.shape
    q = (xn @ wq).reshape(b, l, MEM_HEADS, MEM_HEAD_DIM)
    s = jnp.einsum('blhd,bmhd->bhlm', q, mk).astype(jnp.float32) * MEM_HEAD_DIM ** -0.5
    p = jax.nn.softmax(s, axis=-1).astype(mv.dtype)
    o = jnp.einsum('bhlm,bmhd->blhd', p, mv).reshape(b, l, d)
    return (o @ wo).astype(xn.dtype)


def mem_kv(mem, g, wk, wv):
    b = mem.shape[0]
    mn = rmsnorm(mem, g)
    mk = (mn @ wk).reshape(b, MEM_LEN, MEM_HEADS, MEM_HEAD_DIM)
    mv = (mn @ wv).reshape(b, MEM_LEN, MEM_HEADS, MEM_HEAD_DIM)
    return mk, mv


def hier_moe(xn, lp):
    b, l, d = xn.shape
    t = xn.reshape(b * l, d)
    lg = (t @ lp['moe_w_group']).astype(jnp.float32) + lp['moe_b_group']
    g_idx = jnp.argmax(lg, axis=-1)
    p_group = jnp.take_along_axis(jax.nn.softmax(lg, axis=-1), g_idx[:, None], axis=-1)
    le = ((t @ lp['moe_w_expert']).astype(jnp.float32) + lp['moe_b_expert']).reshape(-1, N_GROUPS, EXPERTS_PER_GROUP)
    le = jnp.take_along_axis(le, g_idx[:, None, None], axis=1)[:, 0]
    top_v, top_i = lax.top_k(jax.nn.softmax(le, axis=-1), TOP_K_INNER)
    w = p_group * top_v / jnp.sum(top_v, axis=-1, keepdims=True)
    eid = g_idx[:, None] * EXPERTS_PER_GROUP + top_i
    combine = jnp.sum(jax.nn.one_hot(eid, N_EXPERTS, dtype=jnp.float32) * w[..., None], axis=1)
    h = jax.nn.silu(jnp.einsum('td,edf->tef', t, lp['moe_w_gate'])) * jnp.einsum('td,edf->tef', t, lp['moe_w_up'])
    y = jnp.einsum('tef,efd->td', h * combine[..., None].astype(h.dtype), lp['moe_w_down'])
    return y.reshape(b, l, d).astype(xn.dtype)


def setup_inputs(seed: int = 0) -> dict:
    key = jax.random.key(seed)
    keys = iter(jax.random.split(key, 64))

    def nrm(shape, scale):
        return jax.random.normal(next(keys), shape, jnp.float32) * scale

    def gain(shape):
        return 1.0 + nrm(shape, 0.02)

    nbuf = min(WINDOW, PAST_LEN)
    dt0 = jnp.exp(jax.random.uniform(next(keys), (DEPTH, SSD_HEADS), jnp.float32, math.log(1e-3), math.log(1e-1)))
    a_init = jax.random.uniform(next(keys), (DEPTH, SSD_HEADS), jnp.float32, 1.0, 16.0)
    dsc = D_MODEL ** -0.5
    return {
        'x_prompt': nrm((BATCH, SEQ, D_MODEL), 1.0),
        'x_sample': nrm((DEC_BATCH, DEC_SEQ, D_MODEL), 1.0),
        'mem_prompt': nrm((BATCH, MEM_LEN, D_MODEL), 1.0),
        'cache_mem_k': nrm((DEPTH, DEC_BATCH, MEM_LEN, MEM_HEADS, MEM_HEAD_DIM), 1.0),
        'cache_mem_v': nrm((DEPTH, DEC_BATCH, MEM_LEN, MEM_HEADS, MEM_HEAD_DIM), 1.0),
        'cache_swa_k': nrm((DEPTH, DEC_BATCH, nbuf, SWA_KV_HEADS, SWA_HEAD_DIM), 1.0),
        'cache_swa_v': nrm((DEPTH, DEC_BATCH, nbuf, SWA_KV_HEADS, SWA_HEAD_DIM), 1.0),
        'state_gla': nrm((DEPTH, DEC_BATCH, GLA_HEADS, GLA_DK, GLA_DV), 0.3),
        'state_ssd': nrm((DEPTH, DEC_BATCH, SSD_HEADS, SSD_HEADDIM, SSD_STATE), 0.3),
        'state_conv': nrm((DEPTH, DEC_BATCH, SSD_CONV - 1, SSD_CONV_DIM), 1.0),
        'norm_mix': gain((DEPTH, D_MODEL)),
        'w_in': nrm((DEPTH, D_MODEL, IN_DIM), dsc),
        'gla_w_a2': nrm((DEPTH, GLA_LOWRANK, GLA_HEADS * GLA_DK), GLA_LOWRANK ** -0.5),
        'gla_b_a': nrm((DEPTH, GLA_HEADS * GLA_DK), 0.1),
        'gla_norm': gain((DEPTH, GLA_DV)),
        'ssd_conv_w': nrm((DEPTH, SSD_CONV, SSD_CONV_DIM), SSD_CONV ** -0.5),
        'ssd_conv_b': nrm((DEPTH, SSD_CONV_DIM), 0.02),
        'ssd_dt_bias': dt0 + jnp.log(-jnp.expm1(-dt0)),
        'ssd_a_log': jnp.log(a_init),
        'ssd_d': gain((DEPTH, SSD_HEADS)),
        'ssd_norm': gain((DEPTH, SSD_INNER)),
        'swa_sinks': nrm((DEPTH, SWA_HEADS), 0.5),
        'w_br_gla': nrm((DEPTH, GLA_HEADS * GLA_DV, D_MODEL), (GLA_HEADS * GLA_DV) ** -0.5),
        'w_br_ssd': nrm((DEPTH, SSD_INNER, D_MODEL), SSD_INNER ** -0.5),
        'w_br_swa': nrm((DEPTH, SWA_HEADS * SWA_HEAD_DIM, D_MODEL), (SWA_HEADS * SWA_HEAD_DIM) ** -0.5),
        'w_out': nrm((DEPTH, D_MODEL, D_MODEL), dsc),
        'norm_mem': gain((DEPTH, D_MODEL)),
        'norm_memkv': gain((DEPTH, D_MODEL)),
        'mem_wq': nrm((DEPTH, D_MODEL, D_MODEL), dsc),
        'mem_wk': nrm((DEPTH, D_MODEL, D_MODEL), dsc),
        'mem_wv': nrm((DEPTH, D_MODEL, D_MODEL), dsc),
        'mem_wo': nrm((DEPTH, D_MODEL, D_MODEL), dsc),
        'norm_ffn': gain((DEPTH, D_MODEL)),
        'moe_w_group': nrm((DEPTH, D_MODEL, N_GROUPS), dsc),
        'moe_b_group': nrm((DEPTH, N_GROUPS), 0.01),
        'moe_w_expert': nrm((DEPTH, D_MODEL, N_EXPERTS), dsc),
        'moe_b_expert': nrm((DEPTH, N_EXPERTS), 0.01),
        'moe_w_gate': nrm((DEPTH, N_EXPERTS, D_MODEL, EXPERT_FF), dsc),
        'moe_w_up': nrm((DEPTH, N_EXPERTS, D_MODEL, EXPERT_FF), dsc),
        'moe_w_down': nrm((DEPTH, N_EXPERTS, EXPERT_FF, D_MODEL), EXPERT_FF ** -0.5),
        'norm_final': gain((D_MODEL,)),
    }


def reference(x_prompt, x_sample, mem_prompt, cache_mem_k, cache_mem_v, cache_swa_k, cache_swa_v,
              state_gla, state_ssd, state_conv, norm_mix, w_in, gla_w_a2, gla_b_a, gla_norm,
              ssd_conv_w, ssd_conv_b, ssd_dt_bias, ssd_a_log, ssd_d, ssd_norm, swa_sinks,
              w_br_gla, w_br_ssd, w_br_swa, w_out, norm_mem, norm_memkv, mem_wq, mem_wk, mem_wv, mem_wo,
              norm_ffn, moe_w_group, moe_b_group, moe_w_expert, moe_b_expert, moe_w_gate, moe_w_up,
              moe_w_down, norm_final):
    hp, hs = x_prompt, x_sample
    bp = x_prompt.shape[0]
    p_mk, p_mv, p_sk, p_sv, p_gla, p_ssd, p_conv = [], [], [], [], [], [], []
    s_sk, s_sv, s_gla, s_ssd, s_conv = [], [], [], [], []
    for l in range(DEPTH):
        lp = dict(w_in=w_in[l], gla_w_a2=gla_w_a2[l], gla_b_a=gla_b_a[l], gla_norm=gla_norm[l],
                  ssd_conv_w=ssd_conv_w[l], ssd_conv_b=ssd_conv_b[l], ssd_dt_bias=ssd_dt_bias[l],
                  ssd_a_log=ssd_a_log[l], ssd_d=ssd_d[l], ssd_norm=ssd_norm[l], swa_sinks=swa_sinks[l],
                  w_br_gla=w_br_gla[l], w_br_ssd=w_br_ssd[l], w_br_swa=w_br_swa[l], w_out=w_out[l],
                  moe_w_group=moe_w_group[l], moe_b_group=moe_b_group[l], moe_w_expert=moe_w_expert[l],
                  moe_b_expert=moe_b_expert[l], moe_w_gate=moe_w_gate[l], moe_w_up=moe_w_up[l],
                  moe_w_down=moe_w_down[l])
        mix, g_s, h_s, c_s, k_s, v_s = mixer(
            rmsnorm(hp, norm_mix[l]), lp,
            jnp.zeros((bp, GLA_HEADS, GLA_DK, GLA_DV), jnp.float32),
            jnp.zeros((bp, SSD_HEADS, SSD_HEADDIM, SSD_STATE), jnp.float32),
            jnp.zeros((bp, SSD_CONV - 1, SSD_CONV_DIM), hp.dtype), None, None)
        hp = hp + mix
        mk, mv = mem_kv(mem_prompt, norm_memkv[l], mem_wk[l], mem_wv[l])
        hp = hp + mem_attn(rmsnorm(hp, norm_mem[l]), mk, mv, mem_wq[l], mem_wo[l])
        hp = hp + hier_moe(rmsnorm(hp, norm_ffn[l]), lp)
        p_mk.append(mk); p_mv.append(mv); p_sk.append(k_s); p_sv.append(v_s)
        p_gla.append(g_s); p_ssd.append(h_s); p_conv.append(c_s)
        mix, g_s, h_s, c_s, k_s, v_s = mixer(
            rmsnorm(hs, norm_mix[l]), lp, state_gla[l], state_ssd[l], state_conv[l],
            cache_swa_k[l], cache_swa_v[l])
        hs = hs + mix
        hs = hs + mem_attn(rmsnorm(hs, norm_mem[l]), cache_mem_k[l], cache_mem_v[l], mem_wq[l], mem_wo[l])
        hs = hs + hier_moe(rmsnorm(hs, norm_ffn[l]), lp)
        s_sk.append(k_s); s_sv.append(v_s); s_gla.append(g_s); s_ssd.append(h_s); s_conv.append(c_s)
    y_prompt = rmsnorm(hp, norm_final)
    y_sample = rmsnorm(hs, norm_final)
    return (y_prompt, y_sample,
            jnp.stack(p_mk), jnp.stack(p_mv), jnp.stack(p_sk), jnp.stack(p_sv),
            jnp.stack(p_gla), jnp.stack(p_ssd), jnp.stack(p_conv),
            jnp.stack(s_sk), jnp.stack(s_sv), jnp.stack(s_gla), jnp.stack(s_ssd), jnp.stack(s_conv))
```

```python
import functools

import jax
import jax.numpy as jnp
from jax import lax
from jax.experimental import pallas as pl
from jax.experimental.pallas import tpu as pltpu

f32 = jnp.float32
bf16 = jnp.bfloat16

GLA_HEADS = 4
GLA_DK = 128
GLA_DV = 128
GLA_LOWRANK = 16
GLA_TAU = 16.0
SSD_HEADS = 8
SSD_HEADDIM = 64
SSD_GROUPS = 2
SSD_HPG = SSD_HEADS // SSD_GROUPS
SSD_STATE = 128
SSD_CONV = 4
SSD_INNER = SSD_HEADS * SSD_HEADDIM
SSD_CONV_DIM = SSD_INNER + 2 * SSD_GROUPS * SSD_STATE
SWA_HEADS = 8
SWA_KV_HEADS = 2
SWA_REP = SWA_HEADS // SWA_KV_HEADS
SWA_HEAD_DIM = 64
WINDOW = 128
PAST_LEN = 16384
MEM_HEADS = 4
N_GROUPS = 4
EXPERTS_PER_GROUP = 4
N_EXPERTS = N_GROUPS * EXPERTS_PER_GROUP
N_BRANCHES = 3
EPS = 1e-6

GLA_W = GLA_HEADS * GLA_DK
SWA_QW = SWA_HEADS * SWA_HEAD_DIM
SWA_KW = SWA_KV_HEADS * SWA_HEAD_DIM

V7X_LANES = 128
V7X_SUBLANES = 8
V7X_VMEM_LIMIT_BYTES = 56 * 1024 * 1024

TOKEN_TILE = 512
GLA_TILE = 256
GLA_CHUNK = 64
SSD_TILE = 256


def _params(*sem):
    return pltpu.CompilerParams(dimension_semantics=sem, vmem_limit_bytes=V7X_VMEM_LIMIT_BYTES)


def _resident(shape):
    n = len(shape)
    return pl.BlockSpec(shape, lambda *_: (0,) * n, pipeline_mode=pl.Buffered(1))


def _dot(a, b):
    return jnp.dot(a.astype(bf16), b.astype(bf16), preferred_element_type=f32)


def _dot_nt(a, b):
    return lax.dot_general(a.astype(bf16), b.astype(bf16), (((1,), (1,)), ((), ())),
                           preferred_element_type=f32)


def _dot_tn(a, b):
    return lax.dot_general(a.astype(bf16), b.astype(bf16), (((0,), (0,)), ((), ())),
                           preferred_element_type=f32)


def _split3(x):
    hi = x.astype(bf16)
    r = x - hi.astype(f32)
    mid = r.astype(bf16)
    lo = (r - mid.astype(f32)).astype(bf16)
    return hi, mid, lo


def _dot_exact_lhs(m_bf16, x):
    hi, mid, lo = _split3(x)
    d = lambda v: jnp.dot(m_bf16, v, preferred_element_type=f32)
    return d(hi) + d(mid) + d(lo)


def _rms(x, g):
    return x * lax.rsqrt(jnp.mean(x * x, axis=-1, keepdims=True) + EPS) * g


def _sigmoid(x):
    return jax.nn.sigmoid(x)


def _silu(x):
    return x * jax.nn.sigmoid(x)


def _softplus(x):
    return jnp.maximum(x, 0.0) + jnp.log1p(jnp.exp(-jnp.abs(x)))


def _log_sigmoid(x):
    return -_softplus(-x)


def _row_to_col(row):
    n = row.shape[1]
    return jnp.broadcast_to(row, (V7X_SUBLANES, n)).T[:, 0:1]


def _tri_incl(n):
    r = lax.broadcasted_iota(jnp.int32, (n, n), 0)
    c = lax.broadcasted_iota(jnp.int32, (n, n), 1)
    return r >= c


_P_GLA = 0
_P_GA = 4 * GLA_W
_P_XBC = _P_GA + V7X_LANES
_P_Z = _P_XBC + SSD_CONV_DIM
_P_DT = _P_Z + SSD_INNER
_P_SWA = _P_DT + V7X_LANES
_P_END = _P_SWA + SWA_QW + 2 * SWA_KW
GLA5_W = 5 * GLA_W
SSD_W = SSD_CONV_DIM + SSD_INNER
SWA_W = SWA_QW + 2 * SWA_KW


def _in_kernel(x_ref, g_ref, w_ref, wa2_ref, ba_ref, gla_ref, ssd_ref, dt_ref, swa_ref):
    xb = _rms(x_ref[...], g_ref[...]).astype(bf16)
    mm = lambda lo, hi: jnp.dot(xb, w_ref[:, lo:hi], preferred_element_type=f32)
    gla_ref[:, 0:4 * GLA_W] = mm(_P_GLA, _P_GA)
    ga = mm(_P_GA, _P_XBC)
    gla_ref[:, 4 * GLA_W:] = _log_sigmoid(_dot(ga, wa2_ref[...]) + ba_ref[...]) / GLA_TAU
    ssd_ref[...] = mm(_P_XBC, _P_DT)
    dt_ref[...] = mm(_P_DT, _P_SWA)
    swa_ref[...] = mm(_P_SWA, _P_END)


def _in_proj(x2, g, w_pack, wa2_pad, ba):
    t, d = x2.shape
    tm = min(TOKEN_TILE, t)
    row = lambda w: pl.BlockSpec((tm, w), lambda i: (i, 0))
    return pl.pallas_call(
        _in_kernel,
        grid=(t // tm,),
        in_specs=[row(d), _resident((1, d)), _resident(w_pack.shape), _resident(wa2_pad.shape),
                  _resident((1, GLA_W))],
        out_specs=[row(GLA5_W), row(SSD_W), row(V7X_LANES), row(SWA_W)],
        out_shape=[jax.ShapeDtypeStruct((t, GLA5_W), f32), jax.ShapeDtypeStruct((t, SSD_W), f32),
                   jax.ShapeDtypeStruct((t, V7X_LANES), f32), jax.ShapeDtypeStruct((t, SWA_W), f32)],
        compiler_params=_params("parallel"),
        name="in_proj",
    )(x2, g, w_pack, wa2_pad, ba)


def _gla_kernel(*refs, chunk, has_init):
    if has_init:
        q_ref, k_ref, v_ref, r_ref, la_ref, gn_ref, s0_ref, o_ref, sout_ref, s_scr = refs
    else:
        q_ref, k_ref, v_ref, r_ref, la_ref, gn_ref, o_ref, sout_ref, s_scr = refs
    i = pl.program_id(1)

    @pl.when(i == 0)
    def _():
        s_scr[...] = s0_ref[0] if has_init else jnp.zeros(s_scr.shape, f32)

    tl = q_ref.shape[1]
    causal = _tri_incl(chunk)
    tri = causal.astype(bf16)

    def body(c, carry):
        rows = pl.ds(pl.multiple_of(c * chunk, chunk), chunk)
        bc = _dot_exact_lhs(tri, la_ref[0, rows, :])
        b_end = bc[chunk - 1:chunk, :]
        k = k_ref[0, rows, :]
        v = v_ref[0, rows, :]
        qt = q_ref[0, rows, :] * (GLA_DK ** -0.5) * jnp.exp(bc)
        kt = k * jnp.exp(-bc)
        kh = k * jnp.exp(b_end - bc)
        e_end = jnp.exp(b_end)
        for h in range(GLA_HEADS):
            sl = slice(h * GLA_DK, (h + 1) * GLA_DK)
            s_old = s_scr[h]
            scores = jnp.where(causal, _dot_nt(qt[:, sl], kt[:, sl]), 0.0)
            o = _dot(scores, v[:, sl]) + _dot(qt[:, sl], s_old)
            s_scr[h] = _row_to_col(e_end[:, sl]) * s_old + _dot_tn(kh[:, sl], v[:, sl])
            o_ref[0, rows, sl] = _rms(o, gn_ref[...]) * _silu(r_ref[0, rows, sl])
        return carry

    lax.fori_loop(0, tl // chunk, body, 0)

    @pl.when(i == pl.num_programs(1) - 1)
    def _():
        sout_ref[0] = s_scr[...]


def _gla(gla5, gn, s0):
    b, l, _ = gla5.shape
    tl = min(GLA_TILE, l)
    chunk = min(GLA_CHUNK, l)
    col = lambda j: pl.BlockSpec((1, tl, GLA_W), lambda bi, i: (bi, i, j))
    st = pl.BlockSpec((1, GLA_HEADS, GLA_DK, GLA_DV), lambda bi, i: (bi, 0, 0, 0))
    in_specs = [col(0), col(1), col(2), col(3), col(4), _resident((1, GLA_DV))]
    args = [gla5] * 5 + [gn]
    if s0 is not None:
        in_specs.append(st)
        args.append(s0)
    return pl.pallas_call(
        functools.partial(_gla_kernel, chunk=chunk, has_init=s0 is not None),
        grid=(b, l // tl),
        in_specs=in_specs,
        out_specs=[col(0), st],
        out_shape=[jax.ShapeDtypeStruct((b, l, GLA_W), f32),
                   jax.ShapeDtypeStruct((b, GLA_HEADS, GLA_DK, GLA_DV), f32)],
        scratch_shapes=[pltpu.VMEM((GLA_HEADS, GLA_DK, GLA_DV), f32)],
        compiler_params=_params("parallel", "arbitrary"),
        name="gla",
    )(*args)


_TAIL_ROWS = V7X_SUBLANES


def _ssd_kernel(*refs, has_init):
    if has_init:
        (xbc_ref, z_ref, dt_ref, cw_ref, cb_ref, dtb_ref, alog_ref, d_ref, nrm_ref, h0_ref, cbuf_ref,
         y_ref, hout_ref, cout_ref, h_scr, tail_scr) = refs
    else:
        (xbc_ref, z_ref, dt_ref, cw_ref, cb_ref, dtb_ref, alog_ref, d_ref, nrm_ref,
         y_ref, hout_ref, cout_ref, h_scr, tail_scr) = refs
    i = pl.program_id(1)
    last = pl.num_programs(1) - 1
    nk = SSD_CONV - 1

    @pl.when(i == 0)
    def _():
        h_scr[...] = h0_ref[0] if has_init else jnp.zeros(h_scr.shape, f32)
        tail_scr[...] = jnp.zeros(tail_scr.shape, f32)
        if has_init:
            tail_scr[_TAIL_ROWS - nk:, :] = cbuf_ref[0]

    tl = xbc_ref.shape[1]
    raw = xbc_ref[0]
    xp = jnp.concatenate([tail_scr[...], raw], axis=0)
    conv = cb_ref[...]
    for j in range(SSD_CONV):
        o = _TAIL_ROWS - nk + j
        conv = conv + xp[o:o + tl] * cw_ref[j:j + 1, :]
    xc = _silu(conv)
    if tl >= _TAIL_ROWS:
        tail_scr[...] = raw[tl - _TAIL_ROWS:, :]

    @pl.when(i == last)
    def _():
        cout_ref[0] = xp[_TAIL_ROWS + tl - nk:, :]

    dt = _softplus(dt_ref[0] + dtb_ref[...])
    a = -jnp.exp(alog_ref[...])
    causal = _tri_incl(tl)
    cum = _dot_exact_lhs(causal.astype(bf16), dt * a)
    cum_row = cum.T
    dt_row = dt.T
    ys = []
    for g in range(SSD_GROUPS):
        bm = xc[:, SSD_INNER + g * SSD_STATE:SSD_INNER + (g + 1) * SSD_STATE]
        cm = xc[:, SSD_INNER + (SSD_GROUPS + g) * SSD_STATE:SSD_INNER + (SSD_GROUPS + g + 1) * SSD_STATE]
        cb = _dot_nt(cm, bm)
        for r in range(SSD_HPG):
            h = g * SSD_HPG + r
            xh = xc[:, h * SSD_HEADDIM:(h + 1) * SSD_HEADDIM]
            cum_c = cum[:, h:h + 1]
            lmat = jnp.exp(jnp.where(causal, cum_c - cum_row[h:h + 1, :], -jnp.inf))
            w = cb * lmat * dt_row[h:h + 1, :]
            h_old = h_scr[h]
            yh = _dot(w, xh) + _dot_nt(cm, h_old) * jnp.exp(cum_c)
            c_end = cum[tl - 1:tl, h:h + 1]
            wcol = jnp.exp(c_end - cum_c) * dt[:, h:h + 1]
            h_scr[h] = jnp.exp(c_end) * h_old + _dot_tn(xh * wcol, bm)
            ys.append(yh + d_ref[:, h:h + 1] * xh)
    y = jnp.concatenate(ys, axis=1) * _silu(z_ref[0])
    gw = SSD_INNER // SSD_GROUPS
    for g in range(SSD_GROUPS):
        sl = slice(g * gw, (g + 1) * gw)
        y_ref[0, :, sl] = _rms(y[:, sl], nrm_ref[:, sl])

    @pl.when(i == last)
    def _():
        hout_ref[0] = h_scr[...]


def _ssd(ssd, dtp, lw, h0, cbuf):
    b, l, _ = ssd.shape
    tl = min(SSD_TILE, l)
    nk = SSD_CONV - 1
    st = pl.BlockSpec((1, SSD_HEADS, SSD_HEADDIM, SSD_STATE), lambda bi, i: (bi, 0, 0, 0))
    cv = pl.BlockSpec((1, nk, SSD_CONV_DIM), lambda bi, i: (bi, 0, 0))
    in_specs = [pl.BlockSpec((1, tl, SSD_CONV_DIM), lambda bi, i: (bi, i, 0)),
                pl.BlockSpec((1, tl, SSD_INNER), lambda bi, i: (bi, i, SSD_CONV_DIM // SSD_INNER)),
                pl.BlockSpec((1, tl, V7X_LANES), lambda bi, i: (bi, i, 0)),
                _resident((SSD_CONV, SSD_CONV_DIM)), _resident((1, SSD_CONV_DIM)),
                _resident((1, V7X_LANES)), _resident((1, V7X_LANES)), _resident((1, V7X_LANES)),
                _resident((1, SSD_INNER))]
    args = [ssd, ssd, dtp, lw["ssd_conv_w"], lw["ssd_conv_b"], lw["ssd_dt_bias"], lw["ssd_a_log"],
            lw["ssd_d"], lw["ssd_norm"]]
    if h0 is not None:
        in_specs += [st, cv]
        args += [h0, cbuf]
    return pl.pallas_call(
        functools.partial(_ssd_kernel, has_init=h0 is not None),
        grid=(b, l // tl),
        in_specs=in_specs,
        out_specs=[pl.BlockSpec((1, tl, SSD_INNER), lambda bi, i: (bi, i, 0)), st, cv],
        out_shape=[jax.ShapeDtypeStruct((b, l, SSD_INNER), f32),
                   jax.ShapeDtypeStruct((b, SSD_HEADS, SSD_HEADDIM, SSD_STATE), f32),
                   jax.ShapeDtypeStruct((b, nk, SSD_CONV_DIM), f32)],
        scratch_shapes=[pltpu.VMEM((SSD_HEADS, SSD_HEADDIM, SSD_STATE), f32),
                        pltpu.VMEM((_TAIL_ROWS, SSD_CONV_DIM), f32)],
        compiler_params=_params("parallel", "arbitrary"),
        name="ssd",
    )(*args)


def _sink_softmax(parts, sink):
    m = sink
    for s in parts:
        m = jnp.maximum(m, jnp.max(s, axis=-1, keepdims=True))
    ps = [jnp.exp(s - m) for s in parts]
    den = jnp.exp(sink - m)
    for p in ps:
        den = den + jnp.sum(p, axis=-1, keepdims=True)
    return [p / den for p in ps]


def _swa_prompt_kernel(sink_ref, q_ref, kv_ref, kvp_ref, o_ref):
    i = pl.program_id(1)
    w = WINDOW
    q = q_ref[0]
    kv = kv_ref[0]
    kvp = jnp.where(i > 0, kvp_ref[0], 0.0)
    rows = SWA_REP * w
    t = lax.broadcasted_iota(jnp.int32, (rows, 2 * w), 0) % w
    j = lax.broadcasted_iota(jnp.int32, (rows, 2 * w), 1)
    rel = t - j + w
    kpos = i * w + j - w
    mask = (rel >= 0) & (rel < w) & (kpos >= 0)
    rep = lax.broadcasted_iota(jnp.int32, (rows, 1), 0) // w
    for g in range(SWA_KV_HEADS):
        ks = slice(g * SWA_HEAD_DIM, (g + 1) * SWA_HEAD_DIM)
        vs = slice(SWA_KW + g * SWA_HEAD_DIM, SWA_KW + (g + 1) * SWA_HEAD_DIM)
        kc = jnp.concatenate([kvp[:, ks], kv[:, ks]], axis=0)
        vc = jnp.concatenate([kvp[:, vs], kv[:, vs]], axis=0)
        h0 = g * SWA_REP
        q4 = jnp.concatenate([q[:, (h0 + r) * SWA_HEAD_DIM:(h0 + r + 1) * SWA_HEAD_DIM]
                              for r in range(SWA_REP)], axis=0)
        sink = jnp.zeros((rows, 1), f32)
        for r in range(SWA_REP):
            sink = jnp.where(rep == r, sink_ref[h0 + r], sink)
        s = jnp.where(mask, _dot_nt(q4, kc) * (SWA_HEAD_DIM ** -0.5), -jnp.inf)
        (p,) = _sink_softmax([s], sink)
        o = _dot(p, vc)
        for r in range(SWA_REP):
            o_ref[0, :, (h0 + r) * SWA_HEAD_DIM:(h0 + r + 1) * SWA_HEAD_DIM] = o[r * w:(r + 1) * w]


def _swa_prompt(swa, sinks):
    b, l, _ = swa.shape
    w = WINDOW
    kvb = SWA_QW // (2 * SWA_KW)
    return pl.pallas_call(
        _swa_prompt_kernel,
        grid=(b, l // w),
        in_specs=[pl.BlockSpec(memory_space=pltpu.SMEM),
                  pl.BlockSpec((1, w, SWA_QW), lambda bi, i: (bi, i, 0)),
                  pl.BlockSpec((1, w, 2 * SWA_KW), lambda bi, i: (bi, i, kvb)),
                  pl.BlockSpec((1, w, 2 * SWA_KW), lambda bi, i: (bi, jnp.maximum(i - 1, 0), kvb))],
        out_specs=pl.BlockSpec((1, w, SWA_QW), lambda bi, i: (bi, i, 0)),
        out_shape=jax.ShapeDtypeStruct((b, l, SWA_QW), f32),
        compiler_params=_params("parallel", "arbitrary"),
        name="swa_prompt",
    )(sinks, swa, swa, swa)


def _swa_sample_kernel(sink_ref, q_ref, kv_ref, kb_ref, vb_ref, o_ref):
    q = q_ref[0]
    kv = kv_ref[0]
    kb = kb_ref[0]
    vb = vb_ref[0]
    l = q.shape[0]
    nbuf = kb.shape[0]

    def mask(shape, off):
        t = lax.broadcasted_iota(jnp.int32, shape, 0)
        j = lax.broadcasted_iota(jnp.int32, shape, 1) + off
        rel = t - j + nbuf
        return (rel >= 0) & (rel < WINDOW) & (PAST_LEN - nbuf + j >= 0)

    m_buf = mask((l, nbuf), 0)
    m_new = mask((l, l), nbuf)
    scale = SWA_HEAD_DIM ** -0.5
    for g in range(SWA_KV_HEADS):
        ks = slice(g * SWA_HEAD_DIM, (g + 1) * SWA_HEAD_DIM)
        vs = slice(SWA_KW + g * SWA_HEAD_DIM, SWA_KW + (g + 1) * SWA_HEAD_DIM)
        for r in range(SWA_REP):
            h = g * SWA_REP + r
            hs = slice(h * SWA_HEAD_DIM, (h + 1) * SWA_HEAD_DIM)
            qh = q[:, hs]
            s_buf = jnp.where(m_buf, _dot_nt(qh, kb[:, ks]) * scale, -jnp.inf)
            s_new = jnp.where(m_new, _dot_nt(qh, kv[:, ks]) * scale, -jnp.inf)
            sink = jnp.full((l, 1), sink_ref[h], f32)
            p_buf, p_new = _sink_softmax([s_buf, s_new], sink)
            o_ref[0, :, hs] = _dot(p_buf, vb[:, ks]) + _dot(p_new, kv[:, vs])


def _swa_sample(swa, kbuf, vbuf, sinks):
    b, l, _ = swa.shape
    nbuf = kbuf.shape[1]
    kvb = SWA_QW // (2 * SWA_KW)
    return pl.pallas_call(
        _swa_sample_kernel,
        grid=(b,),
        in_specs=[pl.BlockSpec(memory_space=pltpu.SMEM),
                  pl.BlockSpec((1, l, SWA_QW), lambda bi: (bi, 0, 0)),
                  pl.BlockSpec((1, l, 2 * SWA_KW), lambda bi: (bi, 0, kvb)),
                  pl.BlockSpec((1, nbuf, SWA_KW), lambda bi: (bi, 0, 0)),
                  pl.BlockSpec((1, nbuf, SWA_KW), lambda bi: (bi, 0, 0))],
        out_specs=pl.BlockSpec((1, l, SWA_QW), lambda bi: (bi, 0, 0)),
        out_shape=jax.ShapeDtypeStruct((b, l, SWA_QW), f32),
        compiler_params=_params("parallel"),
        name="swa_sample",
    )(sinks, swa, swa, kbuf, vbuf)


def _merge_kernel(x_ref, g_ref, og_ref, os_ref, oa_ref, wgate_ref, wbg_ref, wbs_ref, wba_ref, wout_ref, y_ref):
    x = x_ref[...]
    d = x.shape[1]
    xb = _rms(x, g_ref[...]).astype(bf16)
    merged = None
    for j, (br_ref, w_ref) in enumerate(((og_ref, wbg_ref), (os_ref, wbs_ref), (oa_ref, wba_ref))):
        gate = _sigmoid(jnp.dot(xb, wgate_ref[:, j * d:(j + 1) * d], preferred_element_type=f32))
        term = gate * _dot(br_ref[...], w_ref[...])
        merged = term if merged is None else merged + term
    y_ref[...] = x + _dot(merged, wout_ref[...])


def _merge(x2, g, o_gla, y_ssd, o_swa, lw):
    t, d = x2.shape
    tm = min(TOKEN_TILE, t)
    row = lambda w: pl.BlockSpec((tm, w), lambda i: (i, 0))
    ws = [lw["w_gate"], lw["w_br_gla"], lw["w_br_ssd"], lw["w_br_swa"], lw["w_out"]]
    return pl.pallas_call(
        _merge_kernel,
        grid=(t // tm,),
        in_specs=[row(d), _resident((1, d)), row(GLA_W), row(SSD_INNER), row(SWA_QW)]
                 + [_resident(w.shape) for w in ws],
        out_specs=row(d),
        out_shape=jax.ShapeDtypeStruct((t, d), f32),
        compiler_params=_params("parallel"),
        name="merge",
    )(x2, g, o_gla, y_ssd, o_swa, *ws)


def _rms_matmul_kernel(x_ref, g_ref, w_ref, y_ref):
    y_ref[...] = jnp.dot(_rms(x_ref[...], g_ref[...]).astype(bf16), w_ref[...], preferred_element_type=f32)


def _rms_matmul(x2, g, w):
    t, d = x2.shape
    n = w.shape[1]
    tm = min(TOKEN_TILE, t)
    return pl.pallas_call(
        _rms_matmul_kernel,
        grid=(t // tm,),
        in_specs=[pl.BlockSpec((tm, d), lambda i: (i, 0)), _resident((1, d)), _resident(w.shape)],
        out_specs=pl.BlockSpec((tm, n), lambda i: (i, 0)),
        out_shape=jax.ShapeDtypeStruct((t, n), f32),
        compiler_params=_params("parallel"),
        name="rms_matmul",
    )(x2, g, w)


def _matmul_res_kernel(a_ref, w_ref, x_ref, y_ref):
    y_ref[...] = x_ref[...] + _dot(a_ref[...], w_ref[...])


def _matmul_res(a2, w, x2):
    t, d = x2.shape
    tm = min(TOKEN_TILE, t)
    return pl.pallas_call(
        _matmul_res_kernel,
        grid=(t // tm,),
        in_specs=[pl.BlockSpec((tm, a2.shape[1]), lambda i: (i, 0)), _resident(w.shape),
                  pl.BlockSpec((tm, d), lambda i: (i, 0))],
        out_specs=pl.BlockSpec((tm, d), lambda i: (i, 0)),
        out_shape=jax.ShapeDtypeStruct((t, d), f32),
        compiler_params=_params("parallel"),
        name="matmul_res",
    )(a2, w, x2)


def _mem_attn_kernel(q_ref, k_ref, v_ref, o_ref):
    q = q_ref[0]
    hd = q.shape[1] // MEM_HEADS
    for h in range(MEM_HEADS):
        sl = slice(h * hd, (h + 1) * hd)
        s = _dot_nt(q[:, sl], k_ref[0, :, sl]) * (hd ** -0.5)
        p = jnp.exp(s - jnp.max(s, axis=-1, keepdims=True))
        p = p / jnp.sum(p, axis=-1, keepdims=True)
        o_ref[0, :, sl] = _dot(p, v_ref[0, :, sl])


def _mem_attn(q3, mk3, mv3):
    b, l, d = q3.shape
    m = mk3.shape[1]
    tq = min(TOKEN_TILE, l)
    kv = pl.BlockSpec((1, m, d), lambda bi, i: (bi, 0, 0))
    return pl.pallas_call(
        _mem_attn_kernel,
        grid=(b, l // tq),
        in_specs=[pl.BlockSpec((1, tq, d), lambda bi, i: (bi, i, 0)), kv, kv],
        out_specs=pl.BlockSpec((1, tq, d), lambda bi, i: (bi, i, 0)),
        out_shape=jax.ShapeDtypeStruct((b, l, d), f32),
        compiler_params=_params("parallel", "arbitrary"),
        name="mem_attn",
    )(q3, mk3, mv3)


def _moe_kernel(x_ref, g_ref, wrh_ref, wrl_ref, br_ref, wg_ref, wu_ref, wd_ref, gf_ref, o_ref,
                xb_scr, comb_scr, *, final_norm):
    grp = pl.program_id(1)
    tm = x_ref.shape[0]
    lane = lax.broadcasted_iota(jnp.int32, (tm, V7X_LANES), 1)
    lane_f = lane.astype(f32)
    first = lambda hit: jnp.min(jnp.where(hit, lane_f, float(V7X_LANES)), axis=-1, keepdims=True)

    @pl.when(grp == 0)
    def _():
        xn = _rms(x_ref[...], g_ref[...])
        xh = xn.astype(bf16)
        xl = (xn - xh.astype(f32)).astype(bf16)
        xb_scr[...] = xh
        d = lambda a, w: jnp.dot(a, w[...], preferred_element_type=f32)
        logits = d(xh, wrh_ref) + (d(xl, wrh_ref) + d(xh, wrl_ref)) + br_ref[...]
        lg = jnp.where(lane < N_GROUPS, logits[:, :V7X_LANES], -jnp.inf)
        mg = jnp.max(lg, axis=-1, keepdims=True)
        gi = first(lg == mg)
        pg = 1.0 / jnp.sum(jnp.exp(lg - mg), axis=-1, keepdims=True)
        in_grp = ((lane // EXPERTS_PER_GROUP).astype(f32) == gi) & (lane < N_EXPERTS)
        le = jnp.where(in_grp, logits[:, V7X_LANES:], -jnp.inf)
        m1 = jnp.max(le, axis=-1, keepdims=True)
        i1 = first(le == m1)
        z = jnp.sum(jnp.exp(le - m1), axis=-1, keepdims=True)
        le2 = jnp.where(lane_f == i1, -jnp.inf, le)
        m2 = jnp.max(le2, axis=-1, keepdims=True)
        i2 = first(le2 == m2)
        v1 = 1.0 / z
        v2 = jnp.exp(m2 - m1) / z
        tot = v1 + v2
        comb_scr[...] = jnp.where(lane_f == i1, pg * v1 / tot, jnp.where(lane_f == i2, pg * v2 / tot, 0.0))
        o_ref[...] = jnp.zeros(o_ref.shape, f32)

    xb = xb_scr[...]
    comb = comb_scr[...]
    acc = jnp.zeros(o_ref.shape, f32)
    for e in range(EXPERTS_PER_GROUP):
        ce = jnp.sum(jnp.where(lane == grp * EXPERTS_PER_GROUP + e, comb, 0.0), axis=-1, keepdims=True)
        hg = jnp.dot(xb, wg_ref[e], preferred_element_type=f32)
        hu = jnp.dot(xb, wu_ref[e], preferred_element_type=f32)
        acc = acc + _dot(_silu(hg) * hu * ce, wd_ref[e])
    o_ref[...] += acc

    @pl.when(grp == pl.num_programs(1) - 1)
    def _():
        y = x_ref[...] + o_ref[...]
        o_ref[...] = _rms(y, gf_ref[...]) if final_norm else y


def _moe(x2, g, lw, g_final, final_norm):
    t, d = x2.shape
    tm = min(TOKEN_TILE, t)
    ff = lw["moe_w_gate"].shape[2]
    e = EXPERTS_PER_GROUP
    return pl.pallas_call(
        functools.partial(_moe_kernel, final_norm=final_norm),
        grid=(t // tm, N_GROUPS),
        in_specs=[pl.BlockSpec((tm, d), lambda i, j: (i, 0)), _resident((1, d)),
                  _resident(lw["moe_wr_hi"].shape), _resident(lw["moe_wr_lo"].shape),
                  _resident((1, 2 * V7X_LANES)),
                  pl.BlockSpec((e, d, ff), lambda i, j: (j, 0, 0)),
                  pl.BlockSpec((e, d, ff), lambda i, j: (j, 0, 0)),
                  pl.BlockSpec((e, ff, d), lambda i, j: (j, 0, 0)),
                  _resident((1, d))],
        out_specs=pl.BlockSpec((tm, d), lambda i, j: (i, 0)),
        out_shape=jax.ShapeDtypeStruct((t, d), f32),
        scratch_shapes=[pltpu.VMEM((tm, d), bf16), pltpu.VMEM((tm, V7X_LANES), f32)],
        compiler_params=_params("parallel", "arbitrary"),
        name="moe",
    )(x2, g, lw["moe_wr_hi"], lw["moe_wr_lo"], lw["moe_br"], lw["moe_w_gate"], lw["moe_w_up"],
      lw["moe_w_down"], g_final)


def _pad_cols(w, n):
    return jnp.pad(w, ((0, 0), (0, n - w.shape[1])))


def _pad_lanes(v):
    return _pad_cols(v.reshape(1, -1), V7X_LANES)


def _layer_weights(p, l):
    w_in = p["w_in"][l]
    sizes = (GLA_W, GLA_W, GLA_W, GLA_W, GLA_LOWRANK, SSD_INNER, SSD_CONV_DIM, SSD_HEADS,
             SWA_QW, SWA_KW, SWA_KW)
    offs, o = [], 0
    for n in sizes:
        offs.append(o)
        o += n
    seg = lambda idx: w_in[:, offs[idx]:offs[idx] + sizes[idx]]
    gq, gk, gv, gr, ga, sz, sxbc, sdt, aq, ak, av = (seg(i) for i in range(len(sizes)))
    w_pack = jnp.concatenate([gq, gk, gv, gr, _pad_cols(ga, V7X_LANES), sxbc, sz,
                              _pad_cols(sdt, V7X_LANES), aq, ak, av], axis=1).astype(bf16)
    d = w_in.shape[0]
    router = jnp.concatenate([_pad_cols(p["moe_w_group"][l], V7X_LANES),
                              _pad_cols(p["moe_w_expert"][l], V7X_LANES)], axis=1)
    router_hi = router.astype(bf16)
    return dict(
        w_pack=w_pack,
        w_gate=w_in[:, o:o + N_BRANCHES * d].astype(bf16),
        wa2=jnp.pad(p["gla_w_a2"][l], ((0, V7X_LANES - GLA_LOWRANK), (0, 0))).astype(bf16),
        gla_b_a=p["gla_b_a"][l].reshape(1, -1),
        gla_norm=p["gla_norm"][l].reshape(1, -1),
        ssd_conv_w=p["ssd_conv_w"][l],
        ssd_conv_b=p["ssd_conv_b"][l].reshape(1, -1),
        ssd_dt_bias=_pad_lanes(p["ssd_dt_bias"][l]),
        ssd_a_log=_pad_lanes(p["ssd_a_log"][l]),
        ssd_d=_pad_lanes(p["ssd_d"][l]),
        ssd_norm=p["ssd_norm"][l].reshape(1, -1),
        swa_sinks=p["swa_sinks"][l],
        w_br_gla=p["w_br_gla"][l].astype(bf16),
        w_br_ssd=p["w_br_ssd"][l].astype(bf16),
        w_br_swa=p["w_br_swa"][l].astype(bf16),
        w_out=p["w_out"][l].astype(bf16),
        norm_mix=p["norm_mix"][l].reshape(1, -1),
        norm_mem=p["norm_mem"][l].reshape(1, -1),
        norm_memkv=p["norm_memkv"][l].reshape(1, -1),
        norm_ffn=p["norm_ffn"][l].reshape(1, -1),
        mem_wq=p["mem_wq"][l].astype(bf16),
        mem_wk=p["mem_wk"][l].astype(bf16),
        mem_wv=p["mem_wv"][l].astype(bf16),
        mem_wo=p["mem_wo"][l].astype(bf16),
        moe_wr_hi=router_hi,
        moe_wr_lo=(router - router_hi.astype(f32)).astype(bf16),
        moe_br=jnp.concatenate([_pad_lanes(p["moe_b_group"][l]), _pad_lanes(p["moe_b_expert"][l])], axis=1),
        moe_w_gate=p["moe_w_gate"][l].astype(bf16),
        moe_w_up=p["moe_w_up"][l].astype(bf16),
        moe_w_down=p["moe_w_down"][l].astype(bf16),
    )


def _mixer(h3, lw, gla_s0, ssd_h0, conv_buf, swa_kbuf, swa_vbuf):
    b, l, d = h3.shape
    x2 = h3.reshape(b * l, d)
    gla5, ssd, dtp, swa = _in_proj(x2, lw["norm_mix"], lw["w_pack"], lw["wa2"], lw["gla_b_a"])
    o_gla, gla_s = _gla(gla5.reshape(b, l, GLA5_W), lw["gla_norm"], gla_s0)
    y_ssd, ssd_h, conv_s = _ssd(ssd.reshape(b, l, SSD_W), dtp.reshape(b, l, V7X_LANES), lw, ssd_h0, conv_buf)
    swa3 = swa.reshape(b, l, SWA_W)
    k_new = swa3[:, :, SWA_QW:SWA_QW + SWA_KW]
    v_new = swa3[:, :, SWA_QW + SWA_KW:]
    if swa_kbuf is None:
        o_swa = _swa_prompt(swa3, lw["swa_sinks"])
        nkeep = min(WINDOW, l)
        swa_k, swa_v = k_new[:, l - nkeep:], v_new[:, l - nkeep:]
    else:
        nbuf = swa_kbuf.shape[1]
        kb = swa_kbuf.reshape(b, nbuf, SWA_KW)
        vb = swa_vbuf.reshape(b, nbuf, SWA_KW)
        o_swa = _swa_sample(swa3, kb, vb, lw["swa_sinks"])
        swa_k = jnp.concatenate([kb, k_new], axis=1)[:, l:]
        swa_v = jnp.concatenate([vb, v_new], axis=1)[:, l:]
    kv_shape = (b, -1, SWA_KV_HEADS, SWA_HEAD_DIM)
    y2 = _merge(x2, lw["norm_mix"], o_gla.reshape(b * l, GLA_W), y_ssd.reshape(b * l, SSD_INNER),
                o_swa.reshape(b * l, SWA_QW), lw)
    return y2, gla_s, ssd_h, conv_s, swa_k.reshape(kv_shape), swa_v.reshape(kv_shape)


def _mem_block(x2, b, lw, mk3, mv3):
    t, d = x2.shape
    q = _rms_matmul(x2, lw["norm_mem"], lw["mem_wq"])
    o = _mem_attn(q.reshape(b, t // b, d), mk3, mv3)
    return _matmul_res(o.reshape(t, d), lw["mem_wo"], x2)


def kernel(x_prompt, x_sample, mem_prompt, cache_mem_k, cache_mem_v, cache_swa_k, cache_swa_v, state_gla, state_ssd, state_conv, norm_mix, w_in, gla_w_a2, gla_b_a, gla_norm, ssd_conv_w, ssd_conv_b, ssd_dt_bias, ssd_a_log, ssd_d, ssd_norm, swa_sinks, w_br_gla, w_br_ssd, w_br_swa, w_out, norm_mem, norm_memkv, mem_wq, mem_wk, mem_wv, mem_wo, norm_ffn, moe_w_group, moe_b_group, moe_w_expert, moe_b_expert, moe_w_gate, moe_w_up, moe_w_down, norm_final):
    p = dict(norm_mix=norm_mix, w_in=w_in, gla_w_a2=gla_w_a2, gla_b_a=gla_b_a, gla_norm=gla_norm,
             ssd_conv_w=ssd_conv_w, ssd_conv_b=ssd_conv_b, ssd_dt_bias=ssd_dt_bias, ssd_a_log=ssd_a_log,
             ssd_d=ssd_d, ssd_norm=ssd_norm, swa_sinks=swa_sinks, w_br_gla=w_br_gla, w_br_ssd=w_br_ssd,
             w_br_swa=w_br_swa, w_out=w_out, norm_mem=norm_mem, norm_memkv=norm_memkv, mem_wq=mem_wq,
             mem_wk=mem_wk, mem_wv=mem_wv, mem_wo=mem_wo, norm_ffn=norm_ffn, moe_w_group=moe_w_group,
             moe_b_group=moe_b_group, moe_w_expert=moe_w_expert, moe_b_expert=moe_b_expert,
             moe_w_gate=moe_w_gate, moe_w_up=moe_w_up, moe_w_down=moe_w_down)
    depth = w_in.shape[0]
    bp, lp, d = x_prompt.shape
    bs, ls, _ = x_sample.shape
    mlen = mem_prompt.shape[1]
    g_final = norm_final.reshape(1, -1)
    mem2 = mem_prompt.reshape(bp * mlen, d)
    hp = x_prompt
    hs = x_sample
    outs = [[] for _ in range(12)]
    for l in range(depth):
        lw = _layer_weights(p, l)
        last = l == depth - 1
        hp2, g_s, h_s, c_s, k_s, v_s = _mixer(hp, lw, None, None, None, None, None)
        mk = _rms_matmul(mem2, lw["norm_memkv"], lw["mem_wk"])
        mv = _rms_matmul(mem2, lw["norm_memkv"], lw["mem_wv"])
        hp2 = _mem_block(hp2, bp, lw, mk.reshape(bp, mlen, d), mv.reshape(bp, mlen, d))
        hp = _moe(hp2, lw["norm_ffn"], lw, g_final, last).reshape(bp, lp, d)
        mem_shape = (bp, mlen, MEM_HEADS, d // MEM_HEADS)
        for lst, val in zip(outs[:7], (mk.reshape(mem_shape), mv.reshape(mem_shape), k_s, v_s, g_s, h_s, c_s)):
            lst.append(val)
        hs2, g_s, h_s, c_s, k_s, v_s = _mixer(hs, lw, state_gla[l], state_ssd[l], state_conv[l],
                                              cache_swa_k[l], cache_swa_v[l])
        hs2 = _mem_block(hs2, bs, lw, cache_mem_k[l].reshape(bs, mlen, d), cache_mem_v[l].reshape(bs, mlen, d))
        hs = _moe(hs2, lw["norm_ffn"], lw, g_final, last).reshape(bs, ls, d)
        for lst, val in zip(outs[7:], (k_s, v_s, g_s, h_s, c_s)):
            lst.append(val)
    return (hp, hs) + tuple(jnp.stack(o) for o in outs)
```

```python
import functools

import jax
import jax.numpy as jnp
from jax import lax
from jax.experimental import pallas as pl
from jax.experimental.pallas import tpu as pltpu

f32 = jnp.float32
bf16 = jnp.bfloat16

GLA_HEADS = 4
GLA_DK = 128
GLA_DV = 128
GLA_LOWRANK = 16
GLA_TAU = 16.0
SSD_HEADS = 8
SSD_HEADDIM = 64
SSD_GROUPS = 2
SSD_HPG = SSD_HEADS // SSD_GROUPS
SSD_STATE = 128
SSD_CONV = 4
SSD_INNER = SSD_HEADS * SSD_HEADDIM
SSD_CONV_DIM = SSD_INNER + 2 * SSD_GROUPS * SSD_STATE
SWA_HEADS = 8
SWA_KV_HEADS = 2
SWA_REP = SWA_HEADS // SWA_KV_HEADS
SWA_HEAD_DIM = 64
WINDOW = 128
PAST_LEN = 16384
MEM_HEADS = 4
N_GROUPS = 4
EXPERTS_PER_GROUP = 4
N_EXPERTS = N_GROUPS * EXPERTS_PER_GROUP
N_BRANCHES = 3
EPS = 1e-6

GLA_W = GLA_HEADS * GLA_DK
SWA_QW = SWA_HEADS * SWA_HEAD_DIM
SWA_KW = SWA_KV_HEADS * SWA_HEAD_DIM

V7X_LANES = 128
V7X_SUBLANES = 8
V7X_VMEM_LIMIT_BYTES = 56 * 1024 * 1024

TOKEN_TILE = 512
GLA_TILE = 256
GLA_CHUNK = 64
SSD_TILE = 256
SHORT_SEQ_BATCH = 8
MEM_CACHE_BATCH = 4


def _params(*sem):
    return pltpu.CompilerParams(dimension_semantics=sem, vmem_limit_bytes=V7X_VMEM_LIMIT_BYTES)


def _resident(shape):
    n = len(shape)
    return pl.BlockSpec(shape, lambda *_: (0,) * n, pipeline_mode=pl.Buffered(1))


def _dot(a, b):
    return jnp.dot(a.astype(bf16), b.astype(bf16), preferred_element_type=f32)


def _dot_nt(a, b):
    return lax.dot_general(a.astype(bf16), b.astype(bf16), (((1,), (1,)), ((), ())),
                           preferred_element_type=f32)


def _dot_tn(a, b):
    return lax.dot_general(a.astype(bf16), b.astype(bf16), (((0,), (0,)), ((), ())),
                           preferred_element_type=f32)


def _split3(x):
    hi = x.astype(bf16)
    r = x - hi.astype(f32)
    mid = r.astype(bf16)
    lo = (r - mid.astype(f32)).astype(bf16)
    return hi, mid, lo


def _dot_exact_lhs(m_bf16, x):
    hi, mid, lo = _split3(x)
    d = lambda v: jnp.dot(m_bf16, v, preferred_element_type=f32)
    return d(hi) + d(mid) + d(lo)


def _rms(x, g):
    return x * lax.rsqrt(jnp.mean(x * x, axis=-1, keepdims=True) + EPS) * g


def _sigmoid(x):
    return jax.nn.sigmoid(x)


def _silu(x):
    return x * jax.nn.sigmoid(x)


def _softplus(x):
    return jnp.maximum(x, 0.0) + jnp.log1p(jnp.exp(-jnp.abs(x)))


def _log_sigmoid(x):
    return -_softplus(-x)


def _row_to_col(row):
    n = row.shape[1]
    return jnp.broadcast_to(row, (V7X_SUBLANES, n)).T[:, 0:1]


def _tri_incl(n):
    r = lax.broadcasted_iota(jnp.int32, (n, n), 0)
    c = lax.broadcasted_iota(jnp.int32, (n, n), 1)
    return r >= c


def _softmax_rows(s):
    p = jnp.exp(s - jnp.max(s, axis=-1, keepdims=True))
    return p / jnp.sum(p, axis=-1, keepdims=True)


def _seqs_per_step(b, l):
    return SHORT_SEQ_BATCH if l < V7X_SUBLANES and b % SHORT_SEQ_BATCH == 0 else 1


_P_GLA = 0
_P_GA = 4 * GLA_W
_P_XBC = _P_GA + V7X_LANES
_P_Z = _P_XBC + SSD_CONV_DIM
_P_DT = _P_Z + SSD_INNER
_P_SWA = _P_DT + V7X_LANES
_P_END = _P_SWA + SWA_QW + 2 * SWA_KW
GLA5_W = 5 * GLA_W
SSD_W = SSD_CONV_DIM + SSD_INNER
SWA_W = SWA_QW + 2 * SWA_KW


def _in_kernel(x_ref, g_ref, w_ref, wa2_ref, ba_ref, gla_ref, ssd_ref, dt_ref, swa_ref):
    xb = _rms(x_ref[...], g_ref[...]).astype(bf16)
    mm = lambda lo, hi: jnp.dot(xb, w_ref[:, lo:hi], preferred_element_type=f32)
    gla_ref[:, 0:4 * GLA_W] = mm(_P_GLA, _P_GA)
    ga = mm(_P_GA, _P_XBC)
    gla_ref[:, 4 * GLA_W:] = _log_sigmoid(_dot(ga, wa2_ref[...]) + ba_ref[...]) / GLA_TAU
    ssd_ref[...] = mm(_P_XBC, _P_DT)
    dt_ref[...] = mm(_P_DT, _P_SWA)
    swa_ref[...] = mm(_P_SWA, _P_END)


def _in_proj(x2, g, w_pack, wa2_pad, ba):
    t, d = x2.shape
    tm = min(TOKEN_TILE, t)
    row = lambda w: pl.BlockSpec((tm, w), lambda i: (i, 0))
    return pl.pallas_call(
        _in_kernel,
        grid=(t // tm,),
        in_specs=[row(d), _resident((1, d)), _resident(w_pack.shape), _resident(wa2_pad.shape),
                  _resident((1, GLA_W))],
        out_specs=[row(GLA5_W), row(SSD_W), row(V7X_LANES), row(SWA_W)],
        out_shape=[jax.ShapeDtypeStruct((t, GLA5_W), f32), jax.ShapeDtypeStruct((t, SSD_W), f32),
                   jax.ShapeDtypeStruct((t, V7X_LANES), f32), jax.ShapeDtypeStruct((t, SWA_W), f32)],
        compiler_params=_params("parallel"),
        name="in_proj",
    )(x2, g, w_pack, wa2_pad, ba)


def _gla_kernel(*refs, chunk, has_init):
    if has_init:
        q_ref, k_ref, v_ref, r_ref, la_ref, gn_ref, s0_ref, o_ref, sout_ref, s_scr = refs
    else:
        q_ref, k_ref, v_ref, r_ref, la_ref, gn_ref, o_ref, sout_ref, s_scr = refs
    i = pl.program_id(1)

    @pl.when(i == 0)
    def _():
        s_scr[...] = s0_ref[...] if has_init else jnp.zeros(s_scr.shape, f32)

    nb, tl = q_ref.shape[0], q_ref.shape[1]
    causal = _tri_incl(chunk)
    tri = causal.astype(bf16)

    for bb in range(nb):
        for c in range(tl // chunk):
            rows = slice(c * chunk, (c + 1) * chunk)
            bc = _dot_exact_lhs(tri, la_ref[bb, rows, :])
            b_end = bc[chunk - 1:chunk, :]
            k = k_ref[bb, rows, :]
            v = v_ref[bb, rows, :]
            qt = q_ref[bb, rows, :] * (GLA_DK ** -0.5) * jnp.exp(bc)
            kt = k * jnp.exp(-bc)
            kh = k * jnp.exp(b_end - bc)
            e_end = jnp.exp(b_end)
            for h in range(GLA_HEADS):
                sl = slice(h * GLA_DK, (h + 1) * GLA_DK)
                s_old = s_scr[bb, h]
                scores = jnp.where(causal, _dot_nt(qt[:, sl], kt[:, sl]), 0.0)
                o = _dot(scores, v[:, sl]) + _dot(qt[:, sl], s_old)
                s_scr[bb, h] = _row_to_col(e_end[:, sl]) * s_old + _dot_tn(kh[:, sl], v[:, sl])
                o_ref[bb, rows, sl] = _rms(o, gn_ref[...]) * _silu(r_ref[bb, rows, sl])

    @pl.when(i == pl.num_programs(1) - 1)
    def _():
        sout_ref[...] = s_scr[...]


def _gla(gla5, gn, s0_all, layer):
    b, l, _ = gla5.shape
    tl = min(GLA_TILE, l)
    chunk = min(GLA_CHUNK, l)
    nb = _seqs_per_step(b, l)
    col = lambda j: pl.BlockSpec((nb, tl, GLA_W), lambda bi, i: (bi, i, j))
    st = pl.BlockSpec((nb, GLA_HEADS, GLA_DK, GLA_DV), lambda bi, i: (bi, 0, 0, 0))
    in_specs = [col(0), col(1), col(2), col(3), col(4), _resident((1, GLA_DV))]
    args = [gla5] * 5 + [gn]
    if s0_all is not None:
        in_specs.append(pl.BlockSpec((None, nb, GLA_HEADS, GLA_DK, GLA_DV), lambda bi, i: (layer, bi, 0, 0, 0)))
        args.append(s0_all)
    return pl.pallas_call(
        functools.partial(_gla_kernel, chunk=chunk, has_init=s0_all is not None),
        grid=(b // nb, l // tl),
        in_specs=in_specs,
        out_specs=[col(0), st],
        out_shape=[jax.ShapeDtypeStruct((b, l, GLA_W), f32),
                   jax.ShapeDtypeStruct((b, GLA_HEADS, GLA_DK, GLA_DV), f32)],
        scratch_shapes=[pltpu.VMEM((nb, GLA_HEADS, GLA_DK, GLA_DV), f32)],
        compiler_params=_params("parallel", "arbitrary"),
        name="gla",
    )(*args)


_TAIL_ROWS = V7X_SUBLANES


def _ssd_kernel(*refs, has_init):
    if has_init:
        (xbc_ref, z_ref, dt_ref, cw_ref, cb_ref, dtb_ref, alog_ref, d_ref, nrm_ref, h0_ref, cbuf_ref,
         y_ref, hout_ref, cout_ref, h_scr, tail_scr) = refs
    else:
        (xbc_ref, z_ref, dt_ref, cw_ref, cb_ref, dtb_ref, alog_ref, d_ref, nrm_ref,
         y_ref, hout_ref, cout_ref, h_scr, tail_scr) = refs
    i = pl.program_id(1)
    last = pl.num_programs(1) - 1
    nk = SSD_CONV - 1

    @pl.when(i == 0)
    def _():
        h_scr[...] = h0_ref[...] if has_init else jnp.zeros(h_scr.shape, f32)
        tail_scr[...] = jnp.zeros(tail_scr.shape, f32)
        if has_init:
            tail_scr[:, _TAIL_ROWS - nk:, :] = cbuf_ref[...]

    nb, tl = xbc_ref.shape[0], xbc_ref.shape[1]
    a = -jnp.exp(alog_ref[...])
    causal = _tri_incl(tl)
    tri = causal.astype(bf16)
    gw = SSD_INNER // SSD_GROUPS
    for bb in range(nb):
        raw = xbc_ref[bb]
        xp = jnp.concatenate([tail_scr[bb], raw], axis=0)
        conv = cb_ref[...]
        for j in range(SSD_CONV):
            o = _TAIL_ROWS - nk + j
            conv = conv + xp[o:o + tl] * cw_ref[j:j + 1, :]
        xc = _silu(conv)
        if tl >= _TAIL_ROWS:
            tail_scr[bb] = raw[tl - _TAIL_ROWS:, :]

        @pl.when(i == last)
        def _():
            cout_ref[bb] = xp[_TAIL_ROWS + tl - nk:, :]

        dt = _softplus(dt_ref[bb] + dtb_ref[...])
        cum = _dot_exact_lhs(tri, dt * a)
        cum_row = cum.T
        dt_row = dt.T
        ys = []
        for g in range(SSD_GROUPS):
            bm = xc[:, SSD_INNER + g * SSD_STATE:SSD_INNER + (g + 1) * SSD_STATE]
            cm = xc[:, SSD_INNER + (SSD_GROUPS + g) * SSD_STATE:SSD_INNER + (SSD_GROUPS + g + 1) * SSD_STATE]
            cb = _dot_nt(cm, bm)
            for r in range(SSD_HPG):
                h = g * SSD_HPG + r
                xh = xc[:, h * SSD_HEADDIM:(h + 1) * SSD_HEADDIM]
                cum_c = cum[:, h:h + 1]
                lmat = jnp.exp(jnp.where(causal, cum_c - cum_row[h:h + 1, :], -jnp.inf))
                w = cb * lmat * dt_row[h:h + 1, :]
                h_old = h_scr[bb, h]
                yh = _dot(w, xh) + _dot_nt(cm, h_old) * jnp.exp(cum_c)
                c_end = cum[tl - 1:tl, h:h + 1]
                wcol = jnp.exp(c_end - cum_c) * dt[:, h:h + 1]
                h_scr[bb, h] = jnp.exp(c_end) * h_old + _dot_tn(xh * wcol, bm)
                ys.append(yh + d_ref[:, h:h + 1] * xh)
        y = jnp.concatenate(ys, axis=1) * _silu(z_ref[bb])
        for g in range(SSD_GROUPS):
            sl = slice(g * gw, (g + 1) * gw)
            y_ref[bb, :, sl] = _rms(y[:, sl], nrm_ref[:, sl])

    @pl.when(i == last)
    def _():
        hout_ref[...] = h_scr[...]


def _ssd(ssd, dtp, lw, h0_all, cbuf_all, layer):
    b, l, _ = ssd.shape
    tl = min(SSD_TILE, l)
    nk = SSD_CONV - 1
    nb = _seqs_per_step(b, l)
    st = pl.BlockSpec((nb, SSD_HEADS, SSD_HEADDIM, SSD_STATE), lambda bi, i: (bi, 0, 0, 0))
    cv = pl.BlockSpec((nb, nk, SSD_CONV_DIM), lambda bi, i: (bi, 0, 0))
    in_specs = [pl.BlockSpec((nb, tl, SSD_CONV_DIM), lambda bi, i: (bi, i, 0)),
                pl.BlockSpec((nb, tl, SSD_INNER), lambda bi, i: (bi, i, SSD_CONV_DIM // SSD_INNER)),
                pl.BlockSpec((nb, tl, V7X_LANES), lambda bi, i: (bi, i, 0)),
                _resident((SSD_CONV, SSD_CONV_DIM)), _resident((1, SSD_CONV_DIM)),
                _resident((1, V7X_LANES)), _resident((1, V7X_LANES)), _resident((1, V7X_LANES)),
                _resident((1, SSD_INNER))]
    args = [ssd, ssd, dtp, lw["ssd_conv_w"], lw["ssd_conv_b"], lw["ssd_dt_bias"], lw["ssd_a_log"],
            lw["ssd_d"], lw["ssd_norm"]]
    if h0_all is not None:
        in_specs += [pl.BlockSpec((None, nb, SSD_HEADS, SSD_HEADDIM, SSD_STATE),
                                  lambda bi, i: (layer, bi, 0, 0, 0)),
                     pl.BlockSpec((None, nb, nk, SSD_CONV_DIM), lambda bi, i: (layer, bi, 0, 0))]
        args += [h0_all, cbuf_all]
    return pl.pallas_call(
        functools.partial(_ssd_kernel, has_init=h0_all is not None),
        grid=(b // nb, l // tl),
        in_specs=in_specs,
        out_specs=[pl.BlockSpec((nb, tl, SSD_INNER), lambda bi, i: (bi, i, 0)), st, cv],
        out_shape=[jax.ShapeDtypeStruct((b, l, SSD_INNER), f32),
                   jax.ShapeDtypeStruct((b, SSD_HEADS, SSD_HEADDIM, SSD_STATE), f32),
                   jax.ShapeDtypeStruct((b, nk, SSD_CONV_DIM), f32)],
        scratch_shapes=[pltpu.VMEM((nb, SSD_HEADS, SSD_HEADDIM, SSD_STATE), f32),
                        pltpu.VMEM((nb, _TAIL_ROWS, SSD_CONV_DIM), f32)],
        compiler_params=_params("parallel", "arbitrary"),
        name="ssd",
    )(*args)


def _sink_softmax(parts, sink):
    m = sink
    for s in parts:
        m = jnp.maximum(m, jnp.max(s, axis=-1, keepdims=True))
    ps = [jnp.exp(s - m) for s in parts]
    den = jnp.exp(sink - m)
    for p in ps:
        den = den + jnp.sum(p, axis=-1, keepdims=True)
    return [p / den for p in ps]


def _stack_rep_heads(q, g):
    h0 = g * SWA_REP
    return jnp.concatenate([q[:, (h0 + r) * SWA_HEAD_DIM:(h0 + r + 1) * SWA_HEAD_DIM]
                            for r in range(SWA_REP)], axis=0)


def _rep_sinks(sink_ref, g, l):
    rep = lax.broadcasted_iota(jnp.int32, (SWA_REP * l, 1), 0) // l
    sink = jnp.zeros((SWA_REP * l, 1), f32)
    for r in range(SWA_REP):
        sink = jnp.where(rep == r, sink_ref[g * SWA_REP + r], sink)
    return sink


def _swa_prompt_kernel(sink_ref, q_ref, kv_ref, kvp_ref, o_ref):
    i = pl.program_id(1)
    w = WINDOW
    q = q_ref[0]
    kv = kv_ref[0]
    kvp = jnp.where(i > 0, kvp_ref[0], 0.0)
    rows = SWA_REP * w
    t = lax.broadcasted_iota(jnp.int32, (rows, 2 * w), 0) % w
    j = lax.broadcasted_iota(jnp.int32, (rows, 2 * w), 1)
    rel = t - j + w
    kpos = i * w + j - w
    mask = (rel >= 0) & (rel < w) & (kpos >= 0)
    for g in range(SWA_KV_HEADS):
        ks = slice(g * SWA_HEAD_DIM, (g + 1) * SWA_HEAD_DIM)
        vs = slice(SWA_KW + g * SWA_HEAD_DIM, SWA_KW + (g + 1) * SWA_HEAD_DIM)
        kc = jnp.concatenate([kvp[:, ks], kv[:, ks]], axis=0)
        vc = jnp.concatenate([kvp[:, vs], kv[:, vs]], axis=0)
        s = jnp.where(mask, _dot_nt(_stack_rep_heads(q, g), kc) * (SWA_HEAD_DIM ** -0.5), -jnp.inf)
        (p,) = _sink_softmax([s], _rep_sinks(sink_ref, g, w))
        o = _dot(p, vc)
        for r in range(SWA_REP):
            h = g * SWA_REP + r
            o_ref[0, :, h * SWA_HEAD_DIM:(h + 1) * SWA_HEAD_DIM] = o[r * w:(r + 1) * w]


def _swa_prompt(swa, sinks):
    b, l, _ = swa.shape
    w = WINDOW
    kvb = SWA_QW // (2 * SWA_KW)
    return pl.pallas_call(
        _swa_prompt_kernel,
        grid=(b, l // w),
        in_specs=[pl.BlockSpec(memory_space=pltpu.SMEM),
                  pl.BlockSpec((1, w, SWA_QW), lambda bi, i: (bi, i, 0)),
                  pl.BlockSpec((1, w, 2 * SWA_KW), lambda bi, i: (bi, i, kvb)),
                  pl.BlockSpec((1, w, 2 * SWA_KW), lambda bi, i: (bi, jnp.maximum(i - 1, 0), kvb))],
        out_specs=pl.BlockSpec((1, w, SWA_QW), lambda bi, i: (bi, i, 0)),
        out_shape=jax.ShapeDtypeStruct((b, l, SWA_QW), f32),
        compiler_params=_params("parallel", "arbitrary"),
        name="swa_prompt",
    )(sinks, swa, swa, swa)


def _swa_sample_kernel(sink_ref, q_ref, kv_ref, ck_ref, cv_ref, o_ref, ko_ref, vo_ref):
    nb, l = q_ref.shape[0], q_ref.shape[1]
    nbuf = ck_ref.shape[1]
    rows = SWA_REP * l

    def mask(ncols, off):
        t = lax.broadcasted_iota(jnp.int32, (rows, ncols), 0) % l
        j = lax.broadcasted_iota(jnp.int32, (rows, ncols), 1) + off
        rel = t - j + nbuf
        return (rel >= 0) & (rel < WINDOW) & (PAST_LEN - nbuf + j >= 0)

    m_buf = mask(nbuf, 0)
    m_new = mask(l, nbuf)
    scale = SWA_HEAD_DIM ** -0.5
    for bb in range(nb):
        q = q_ref[bb]
        kv = kv_ref[bb]
        for g in range(SWA_KV_HEADS):
            kb = ck_ref[bb, :, g, :]
            vb = cv_ref[bb, :, g, :]
            kn = kv[:, g * SWA_HEAD_DIM:(g + 1) * SWA_HEAD_DIM]
            vn = kv[:, SWA_KW + g * SWA_HEAD_DIM:SWA_KW + (g + 1) * SWA_HEAD_DIM]
            q4 = _stack_rep_heads(q, g)
            s_buf = jnp.where(m_buf, _dot_nt(q4, kb) * scale, -jnp.inf)
            s_new = jnp.where(m_new, _dot_nt(q4, kn) * scale, -jnp.inf)
            p_buf, p_new = _sink_softmax([s_buf, s_new], _rep_sinks(sink_ref, g, l))
            o = _dot(p_buf, vb) + _dot(p_new, vn)
            for r in range(SWA_REP):
                h = g * SWA_REP + r
                o_ref[bb, :, h * SWA_HEAD_DIM:(h + 1) * SWA_HEAD_DIM] = o[r * l:(r + 1) * l]
            ko_ref[bb, 0:nbuf - l, g, :] = kb[l:]
            ko_ref[bb, nbuf - l:, g, :] = kn
            vo_ref[bb, 0:nbuf - l, g, :] = vb[l:]
            vo_ref[bb, nbuf - l:, g, :] = vn


def _swa_sample(swa, ck_all, cv_all, sinks, layer):
    b, l, _ = swa.shape
    nbuf = ck_all.shape[2]
    nb = _seqs_per_step(b, l)
    kvb = SWA_QW // (2 * SWA_KW)
    cache_in = pl.BlockSpec((None, nb, nbuf, SWA_KV_HEADS, SWA_HEAD_DIM), lambda bi: (layer, bi, 0, 0, 0))
    cache_out = pl.BlockSpec((nb, nbuf, SWA_KV_HEADS, SWA_HEAD_DIM), lambda bi: (bi, 0, 0, 0))
    cache_shape = jax.ShapeDtypeStruct((b, nbuf, SWA_KV_HEADS, SWA_HEAD_DIM), f32)
    return pl.pallas_call(
        _swa_sample_kernel,
        grid=(b // nb,),
        in_specs=[pl.BlockSpec(memory_space=pltpu.SMEM),
                  pl.BlockSpec((nb, l, SWA_QW), lambda bi: (bi, 0, 0)),
                  pl.BlockSpec((nb, l, 2 * SWA_KW), lambda bi: (bi, 0, kvb)),
                  cache_in, cache_in],
        out_specs=[pl.BlockSpec((nb, l, SWA_QW), lambda bi: (bi, 0, 0)), cache_out, cache_out],
        out_shape=[jax.ShapeDtypeStruct((b, l, SWA_QW), f32), cache_shape, cache_shape],
        compiler_params=_params("parallel"),
        name="swa_sample",
    )(sinks, swa, swa, ck_all, cv_all)


def _merge_kernel(x_ref, g_ref, og_ref, os_ref, oa_ref, wgate_ref, wbg_ref, wbs_ref, wba_ref, wout_ref, y_ref):
    x = x_ref[...]
    d = x.shape[1]
    xb = _rms(x, g_ref[...]).astype(bf16)
    merged = None
    for j, (br_ref, w_ref) in enumerate(((og_ref, wbg_ref), (os_ref, wbs_ref), (oa_ref, wba_ref))):
        gate = _sigmoid(jnp.dot(xb, wgate_ref[:, j * d:(j + 1) * d], preferred_element_type=f32))
        term = gate * _dot(br_ref[...], w_ref[...])
        merged = term if merged is None else merged + term
    y_ref[...] = x + _dot(merged, wout_ref[...])


def _merge(x2, g, o_gla, y_ssd, o_swa, lw):
    t, d = x2.shape
    tm = min(TOKEN_TILE, t)
    row = lambda w: pl.BlockSpec((tm, w), lambda i: (i, 0))
    ws = [lw["w_gate"], lw["w_br_gla"], lw["w_br_ssd"], lw["w_br_swa"], lw["w_out"]]
    return pl.pallas_call(
        _merge_kernel,
        grid=(t // tm,),
        in_specs=[row(d), _resident((1, d)), row(GLA_W), row(SSD_INNER), row(SWA_QW)]
                 + [_resident(w.shape) for w in ws],
        out_specs=row(d),
        out_shape=jax.ShapeDtypeStruct((t, d), f32),
        compiler_params=_params("parallel"),
        name="merge",
    )(x2, g, o_gla, y_ssd, o_swa, *ws)


def _rms_matmul_kernel(x_ref, g_ref, w_ref, y_ref):
    y_ref[...] = jnp.dot(_rms(x_ref[...], g_ref[...]).astype(bf16), w_ref[...], preferred_element_type=f32)


def _rms_matmul(x2, g, w):
    t, d = x2.shape
    n = w.shape[1]
    tm = min(TOKEN_TILE, t)
    return pl.pallas_call(
        _rms_matmul_kernel,
        grid=(t // tm,),
        in_specs=[pl.BlockSpec((tm, d), lambda i: (i, 0)), _resident((1, d)), _resident(w.shape)],
        out_specs=pl.BlockSpec((tm, n), lambda i: (i, 0)),
        out_shape=jax.ShapeDtypeStruct((t, n), f32),
        compiler_params=_params("parallel"),
        name="rms_matmul",
    )(x2, g, w)


def _matmul_res_kernel(a_ref, w_ref, x_ref, y_ref):
    y_ref[...] = x_ref[...] + _dot(a_ref[...], w_ref[...])


def _matmul_res(a2, w, x2):
    t, d = x2.shape
    tm = min(TOKEN_TILE, t)
    return pl.pallas_call(
        _matmul_res_kernel,
        grid=(t // tm,),
        in_specs=[pl.BlockSpec((tm, a2.shape[1]), lambda i: (i, 0)), _resident(w.shape),
                  pl.BlockSpec((tm, d), lambda i: (i, 0))],
        out_specs=pl.BlockSpec((tm, d), lambda i: (i, 0)),
        out_shape=jax.ShapeDtypeStruct((t, d), f32),
        compiler_params=_params("parallel"),
        name="matmul_res",
    )(a2, w, x2)


def _mem_fused_kernel(x_ref, g_ref, wq_ref, k_ref, v_ref, wo_ref, y_ref):
    x = x_ref[...]
    q = jnp.dot(_rms(x, g_ref[...]).astype(bf16), wq_ref[...], preferred_element_type=f32)
    hd = q.shape[1] // MEM_HEADS
    os = []
    for h in range(MEM_HEADS):
        sl = slice(h * hd, (h + 1) * hd)
        p = _softmax_rows(_dot_nt(q[:, sl], k_ref[0, :, sl]) * (hd ** -0.5))
        os.append(_dot(p, v_ref[0, :, sl]))
    y_ref[...] = x + _dot(jnp.concatenate(os, axis=1), wo_ref[...])


def _mem_fused(x2, b, g, wq, wo, mk3, mv3):
    t, d = x2.shape
    l = t // b
    m = mk3.shape[1]
    tq = min(TOKEN_TILE, l)
    nq = l // tq
    row = pl.BlockSpec((tq, d), lambda bi, i: (bi * nq + i, 0))
    kv = pl.BlockSpec((1, m, d), lambda bi, i: (bi, 0, 0))
    return pl.pallas_call(
        _mem_fused_kernel,
        grid=(b, nq),
        in_specs=[row, _resident((1, d)), _resident(wq.shape), kv, kv, _resident(wo.shape)],
        out_specs=row,
        out_shape=jax.ShapeDtypeStruct((t, d), f32),
        compiler_params=_params("parallel", "arbitrary"),
        name="mem_fused",
    )(x2, g, wq, mk3, mv3, wo)


def _mem_cache_attn_kernel(q_ref, k_ref, v_ref, o_ref, kh_scr, vh_scr):
    nb = q_ref.shape[0]
    hd = k_ref.shape[3]
    for bb in range(nb):
        for h in range(MEM_HEADS):
            sl = slice(h * hd, (h + 1) * hd)
            kh_scr[...] = k_ref[bb, :, h, :]
            vh_scr[...] = v_ref[bb, :, h, :]
            p = _softmax_rows(_dot_nt(q_ref[bb, :, sl], kh_scr[...]) * (hd ** -0.5))
            o_ref[bb, :, sl] = _dot(p, vh_scr[...])


def _mem_cache_attn(q3, ck_all, cv_all, layer):
    b, l, d = q3.shape
    _, _, m, nh, hd = ck_all.shape
    nb = MEM_CACHE_BATCH if b % MEM_CACHE_BATCH == 0 else 1
    cache = pl.BlockSpec((None, nb, m, nh, hd), lambda bi: (layer, bi, 0, 0, 0))
    return pl.pallas_call(
        _mem_cache_attn_kernel,
        grid=(b // nb,),
        in_specs=[pl.BlockSpec((nb, l, d), lambda bi: (bi, 0, 0)), cache, cache],
        out_specs=pl.BlockSpec((nb, l, d), lambda bi: (bi, 0, 0)),
        out_shape=jax.ShapeDtypeStruct((b, l, d), f32),
        scratch_shapes=[pltpu.VMEM((m, hd), f32), pltpu.VMEM((m, hd), f32)],
        compiler_params=_params("parallel"),
        name="mem_cache_attn",
    )(q3, ck_all, cv_all)


def _moe_kernel(x_ref, g_ref, wrh_ref, wrl_ref, br_ref, wg_ref, wu_ref, wd_ref, gf_ref, o_ref,
                xb_scr, comb_scr, *, final_norm):
    grp = pl.program_id(1)
    tm = x_ref.shape[0]
    lane = lax.broadcasted_iota(jnp.int32, (tm, V7X_LANES), 1)
    lane_f = lane.astype(f32)
    first = lambda hit: jnp.min(jnp.where(hit, lane_f, float(V7X_LANES)), axis=-1, keepdims=True)

    @pl.when(grp == 0)
    def _():
        xn = _rms(x_ref[...], g_ref[...])
        xh = xn.astype(bf16)
        xl = (xn - xh.astype(f32)).astype(bf16)
        xb_scr[...] = xh
        d = lambda a, w: jnp.dot(a, w[...], preferred_element_type=f32)
        logits = d(xh, wrh_ref) + (d(xl, wrh_ref) + d(xh, wrl_ref)) + br_ref[...]
        lg = jnp.where(lane < N_GROUPS, logits[:, :V7X_LANES], -jnp.inf)
        mg = jnp.max(lg, axis=-1, keepdims=True)
        gi = first(lg == mg)
        pg = 1.0 / jnp.sum(jnp.exp(lg - mg), axis=-1, keepdims=True)
        in_grp = ((lane // EXPERTS_PER_GROUP).astype(f32) == gi) & (lane < N_EXPERTS)
        le = jnp.where(in_grp, logits[:, V7X_LANES:], -jnp.inf)
        m1 = jnp.max(le, axis=-1, keepdims=True)
        i1 = first(le == m1)
        z = jnp.sum(jnp.exp(le - m1), axis=-1, keepdims=True)
        le2 = jnp.where(lane_f == i1, -jnp.inf, le)
        m2 = jnp.max(le2, axis=-1, keepdims=True)
        i2 = first(le2 == m2)
        v1 = 1.0 / z
        v2 = jnp.exp(m2 - m1) / z
        tot = v1 + v2
        comb_scr[...] = jnp.where(lane_f == i1, pg * v1 / tot, jnp.where(lane_f == i2, pg * v2 / tot, 0.0))
        o_ref[...] = jnp.zeros(o_ref.shape, f32)

    xb = xb_scr[...]
    comb = comb_scr[...]
    acc = jnp.zeros(o_ref.shape, f32)
    for e in range(EXPERTS_PER_GROUP):
        ce = jnp.sum(jnp.where(lane == grp * EXPERTS_PER_GROUP + e, comb, 0.0), axis=-1, keepdims=True)
        hg = jnp.dot(xb, wg_ref[e], preferred_element_type=f32)
        hu = jnp.dot(xb, wu_ref[e], preferred_element_type=f32)
        acc = acc + _dot(_silu(hg) * hu * ce, wd_ref[e])
    o_ref[...] += acc

    @pl.when(grp == pl.num_programs(1) - 1)
    def _():
        y = x_ref[...] + o_ref[...]
        o_ref[...] = _rms(y, gf_ref[...]) if final_norm else y


def _moe(x2, g, lw, g_final, final_norm):
    t, d = x2.shape
    tm = min(TOKEN_TILE, t)
    ff = lw["moe_w_gate"].shape[2]
    e = EXPERTS_PER_GROUP
    return pl.pallas_call(
        functools.partial(_moe_kernel, final_norm=final_norm),
        grid=(t // tm, N_GROUPS),
        in_specs=[pl.BlockSpec((tm, d), lambda i, j: (i, 0)), _resident((1, d)),
                  _resident(lw["moe_wr_hi"].shape), _resident(lw["moe_wr_lo"].shape),
                  _resident((1, 2 * V7X_LANES)),
                  pl.BlockSpec((e, d, ff), lambda i, j: (j, 0, 0)),
                  pl.BlockSpec((e, d, ff), lambda i, j: (j, 0, 0)),
                  pl.BlockSpec((e, ff, d), lambda i, j: (j, 0, 0)),
                  _resident((1, d))],
        out_specs=pl.BlockSpec((tm, d), lambda i, j: (i, 0)),
        out_shape=jax.ShapeDtypeStruct((t, d), f32),
        scratch_shapes=[pltpu.VMEM((tm, d), bf16), pltpu.VMEM((tm, V7X_LANES), f32)],
        compiler_params=_params("parallel", "arbitrary"),
        name="moe",
    )(x2, g, lw["moe_wr_hi"], lw["moe_wr_lo"], lw["moe_br"], lw["moe_w_gate"], lw["moe_w_up"],
      lw["moe_w_down"], g_final)


def _pad_cols(w, n):
    return jnp.pad(w, ((0, 0), (0, n - w.shape[1])))


def _pad_lanes(v):
    return _pad_cols(v.reshape(1, -1), V7X_LANES)


def _layer_weights(p, l):
    w_in = p["w_in"][l]
    sizes = (GLA_W, GLA_W, GLA_W, GLA_W, GLA_LOWRANK, SSD_INNER, SSD_CONV_DIM, SSD_HEADS,
             SWA_QW, SWA_KW, SWA_KW)
    offs, o = [], 0
    for n in sizes:
        offs.append(o)
        o += n
    seg = lambda idx: w_in[:, offs[idx]:offs[idx] + sizes[idx]]
    gq, gk, gv, gr, ga, sz, sxbc, sdt, aq, ak, av = (seg(i) for i in range(len(sizes)))
    w_pack = jnp.concatenate([gq, gk, gv, gr, _pad_cols(ga, V7X_LANES), sxbc, sz,
                              _pad_cols(sdt, V7X_LANES), aq, ak, av], axis=1).astype(bf16)
    d = w_in.shape[0]
    router = jnp.concatenate([_pad_cols(p["moe_w_group"][l], V7X_LANES),
                              _pad_cols(p["moe_w_expert"][l], V7X_LANES)], axis=1)
    router_hi = router.astype(bf16)
    return dict(
        w_pack=w_pack,
        w_gate=w_in[:, o:o + N_BRANCHES * d].astype(bf16),
        wa2=jnp.pad(p["gla_w_a2"][l], ((0, V7X_LANES - GLA_LOWRANK), (0, 0))).astype(bf16),
        gla_b_a=p["gla_b_a"][l].reshape(1, -1),
        gla_norm=p["gla_norm"][l].reshape(1, -1),
        ssd_conv_w=p["ssd_conv_w"][l],
        ssd_conv_b=p["ssd_conv_b"][l].reshape(1, -1),
        ssd_dt_bias=_pad_lanes(p["ssd_dt_bias"][l]),
        ssd_a_log=_pad_lanes(p["ssd_a_log"][l]),
        ssd_d=_pad_lanes(p["ssd_d"][l]),
        ssd_norm=p["ssd_norm"][l].reshape(1, -1),
        swa_sinks=p["swa_sinks"][l],
        w_br_gla=p["w_br_gla"][l].astype(bf16),
        w_br_ssd=p["w_br_ssd"][l].astype(bf16),
        w_br_swa=p["w_br_swa"][l].astype(bf16),
        w_out=p["w_out"][l].astype(bf16),
        norm_mix=p["norm_mix"][l].reshape(1, -1),
        norm_mem=p["norm_mem"][l].reshape(1, -1),
        norm_memkv=p["norm_memkv"][l].reshape(1, -1),
        norm_ffn=p["norm_ffn"][l].reshape(1, -1),
        mem_wq=p["mem_wq"][l].astype(bf16),
        mem_wk=p["mem_wk"][l].astype(bf16),
        mem_wv=p["mem_wv"][l].astype(bf16),
        mem_wo=p["mem_wo"][l].astype(bf16),
        moe_wr_hi=router_hi,
        moe_wr_lo=(router - router_hi.astype(f32)).astype(bf16),
        moe_br=jnp.concatenate([_pad_lanes(p["moe_b_group"][l]), _pad_lanes(p["moe_b_expert"][l])], axis=1),
        moe_w_gate=p["moe_w_gate"][l].astype(bf16),
        moe_w_up=p["moe_w_up"][l].astype(bf16),
        moe_w_down=p["moe_w_down"][l].astype(bf16),
    )


def _mixer(h3, lw, layer, state):
    b, l, d = h3.shape
    x2 = h3.reshape(b * l, d)
    gla_s0, ssd_h0, conv_buf, swa_kbuf, swa_vbuf = state if state is not None else (None,) * 5
    gla5, ssd, dtp, swa = _in_proj(x2, lw["norm_mix"], lw["w_pack"], lw["wa2"], lw["gla_b_a"])
    o_gla, gla_s = _gla(gla5.reshape(b, l, GLA5_W), lw["gla_norm"], gla_s0, layer)
    y_ssd, ssd_h, conv_s = _ssd(ssd.reshape(b, l, SSD_W), dtp.reshape(b, l, V7X_LANES), lw, ssd_h0, conv_buf,
                                layer)
    swa3 = swa.reshape(b, l, SWA_W)
    if state is None:
        o_swa = _swa_prompt(swa3, lw["swa_sinks"])
        nkeep = min(WINDOW, l)
        kv_shape = (b, nkeep, SWA_KV_HEADS, SWA_HEAD_DIM)
        swa_k = swa3[:, l - nkeep:, SWA_QW:SWA_QW + SWA_KW].reshape(kv_shape)
        swa_v = swa3[:, l - nkeep:, SWA_QW + SWA_KW:].reshape(kv_shape)
    else:
        o_swa, swa_k, swa_v = _swa_sample(swa3, swa_kbuf, swa_vbuf, lw["swa_sinks"], layer)
    y2 = _merge(x2, lw["norm_mix"], o_gla.reshape(b * l, GLA_W), y_ssd.reshape(b * l, SSD_INNER),
                o_swa.reshape(b * l, SWA_QW), lw)
    return y2, gla_s, ssd_h, conv_s, swa_k, swa_v


def kernel(x_prompt, x_sample, mem_prompt, cache_mem_k, cache_mem_v, cache_swa_k, cache_swa_v, state_gla, state_ssd, state_conv, norm_mix, w_in, gla_w_a2, gla_b_a, gla_norm, ssd_conv_w, ssd_conv_b, ssd_dt_bias, ssd_a_log, ssd_d, ssd_norm, swa_sinks, w_br_gla, w_br_ssd, w_br_swa, w_out, norm_mem, norm_memkv, mem_wq, mem_wk, mem_wv, mem_wo, norm_ffn, moe_w_group, moe_b_group, moe_w_expert, moe_b_expert, moe_w_gate, moe_w_up, moe_w_down, norm_final):
    p = dict(norm_mix=norm_mix, w_in=w_in, gla_w_a2=gla_w_a2, gla_b_a=gla_b_a, gla_norm=gla_norm,
             ssd_conv_w=ssd_conv_w, ssd_conv_b=ssd_conv_b, ssd_dt_bias=ssd_dt_bias, ssd_a_log=ssd_a_log,
             ssd_d=ssd_d, ssd_norm=ssd_norm, swa_sinks=swa_sinks, w_br_gla=w_br_gla, w_br_ssd=w_br_ssd,
             w_br_swa=w_br_swa, w_out=w_out, norm_mem=norm_mem, norm_memkv=norm_memkv, mem_wq=mem_wq,
             mem_wk=mem_wk, mem_wv=mem_wv, mem_wo=mem_wo, norm_ffn=norm_ffn, moe_w_group=moe_w_group,
             moe_b_group=moe_b_group, moe_w_expert=moe_w_expert, moe_b_expert=moe_b_expert,
             moe_w_gate=moe_w_gate, moe_w_up=moe_w_up, moe_w_down=moe_w_down)
    depth = w_in.shape[0]
    bp, lp, d = x_prompt.shape
    bs, ls, _ = x_sample.shape
    mlen = mem_prompt.shape[1]
    g_final = norm_final.reshape(1, -1)
    mem2 = mem_prompt.reshape(bp * mlen, d)
    sample_state = (state_gla, state_ssd, state_conv, cache_swa_k, cache_swa_v)
    hp = x_prompt
    hs = x_sample
    outs = [[] for _ in range(12)]
    for l in range(depth):
        lw = _layer_weights(p, l)
        last = l == depth - 1
        hp2, g_s, h_s, c_s, k_s, v_s = _mixer(hp, lw, l, None)
        mk = _rms_matmul(mem2, lw["norm_memkv"], lw["mem_wk"])
        mv = _rms_matmul(mem2, lw["norm_memkv"], lw["mem_wv"])
        hp2 = _mem_fused(hp2, bp, lw["norm_mem"], lw["mem_wq"], lw["mem_wo"],
                         mk.reshape(bp, mlen, d), mv.reshape(bp, mlen, d))
        hp = _moe(hp2, lw["norm_ffn"], lw, g_final, last).reshape(bp, lp, d)
        mem_shape = (bp, mlen, MEM_HEADS, d // MEM_HEADS)
        for lst, val in zip(outs[:7], (mk.reshape(mem_shape), mv.reshape(mem_shape), k_s, v_s, g_s, h_s, c_s)):
            lst.append(val)
        hs2, g_s, h_s, c_s, k_s, v_s = _mixer(hs, lw, l, sample_state)
        q = _rms_matmul(hs2, lw["norm_mem"], lw["mem_wq"])
        o = _mem_cache_attn(q.reshape(bs, ls, d), cache_mem_k, cache_mem_v, l)
        hs2 = _matmul_res(o.reshape(bs * ls, d), lw["mem_wo"], hs2)
        hs = _moe(hs2, lw["norm_ffn"], lw, g_final, last).reshape(bs, ls, d)
        for lst, val in zip(outs[7:], (k_s, v_s, g_s, h_s, c_s)):
            lst.append(val)
    return (hp, hs) + tuple(jnp.stack(o) for o in outs)
```

```python
import functools

import jax
import jax.numpy as jnp
from jax import lax
from jax.experimental import pallas as pl
from jax.experimental.pallas import tpu as pltpu

f32 = jnp.float32
bf16 = jnp.bfloat16

GLA_HEADS = 4
GLA_DK = 128
GLA_DV = 128
GLA_LOWRANK = 16
GLA_TAU = 16.0
SSD_HEADS = 8
SSD_HEADDIM = 64
SSD_GROUPS = 2
SSD_HPG = SSD_HEADS // SSD_GROUPS
SSD_STATE = 128
SSD_CONV = 4
SSD_INNER = SSD_HEADS * SSD_HEADDIM
SSD_CONV_DIM = SSD_INNER + 2 * SSD_GROUPS * SSD_STATE
SWA_HEADS = 8
SWA_KV_HEADS = 2
SWA_REP = SWA_HEADS // SWA_KV_HEADS
SWA_HEAD_DIM = 64
WINDOW = 128
PAST_LEN = 16384
MEM_HEADS = 4
N_GROUPS = 4
EXPERTS_PER_GROUP = 4
N_EXPERTS = N_GROUPS * EXPERTS_PER_GROUP
N_BRANCHES = 3
EPS = 1e-6

GLA_W = GLA_HEADS * GLA_DK
SWA_QW = SWA_HEADS * SWA_HEAD_DIM
SWA_KW = SWA_KV_HEADS * SWA_HEAD_DIM

V7X_LANES = 128
V7X_SUBLANES = 8
V7X_VMEM_LIMIT_BYTES = 56 * 1024 * 1024

TOKEN_TILE = 512
MOE_TILE = 1024
GLA_TILE = 256
GLA_CHUNK = 64
SSD_TILE = 256
SWA_WINDOWS_PER_STEP = 2
SHORT_SEQ_BATCH = 8
MEM_CACHE_BATCH = 4


def _params(*sem):
    return pltpu.CompilerParams(dimension_semantics=sem, vmem_limit_bytes=V7X_VMEM_LIMIT_BYTES)


def _resident(shape):
    n = len(shape)
    return pl.BlockSpec(shape, lambda *_: (0,) * n, pipeline_mode=pl.Buffered(1))


def _dot(a, b):
    return jnp.dot(a.astype(bf16), b.astype(bf16), preferred_element_type=f32)


def _dot_nt(a, b):
    return lax.dot_general(a.astype(bf16), b.astype(bf16), (((1,), (1,)), ((), ())),
                           preferred_element_type=f32)


def _dot_tn(a, b):
    return lax.dot_general(a.astype(bf16), b.astype(bf16), (((0,), (0,)), ((), ())),
                           preferred_element_type=f32)


def _split3(x):
    hi = x.astype(bf16)
    r = x - hi.astype(f32)
    mid = r.astype(bf16)
    lo = (r - mid.astype(f32)).astype(bf16)
    return hi, mid, lo


def _dot_exact_lhs(m_bf16, x):
    hi, mid, lo = _split3(x)
    d = lambda v: jnp.dot(m_bf16, v, preferred_element_type=f32)
    return d(hi) + d(mid) + d(lo)


def _rms(x, g):
    return x * lax.rsqrt(jnp.mean(x * x, axis=-1, keepdims=True) + EPS) * g


def _sigmoid(x):
    return jax.nn.sigmoid(x)


def _silu(x):
    return x * jax.nn.sigmoid(x)


def _softplus(x):
    return jnp.maximum(x, 0.0) + jnp.log1p(jnp.exp(-jnp.abs(x)))


def _log_sigmoid(x):
    return -_softplus(-x)


def _row_to_col(row):
    n = row.shape[1]
    return jnp.broadcast_to(row, (V7X_SUBLANES, n)).T[:, 0:1]


def _tri_incl(n):
    r = lax.broadcasted_iota(jnp.int32, (n, n), 0)
    c = lax.broadcasted_iota(jnp.int32, (n, n), 1)
    return r >= c


def _softmax_rows(s):
    p = jnp.exp(s - jnp.max(s, axis=-1, keepdims=True))
    return p / jnp.sum(p, axis=-1, keepdims=True)


def _seqs_per_step(b, l):
    return SHORT_SEQ_BATCH if l < V7X_SUBLANES and b % SHORT_SEQ_BATCH == 0 else 1


_P_GLA = 0
_P_GA = 4 * GLA_W
_P_XBC = _P_GA + V7X_LANES
_P_Z = _P_XBC + SSD_CONV_DIM
_P_DT = _P_Z + SSD_INNER
_P_SWA = _P_DT + V7X_LANES
_P_END = _P_SWA + SWA_QW + 2 * SWA_KW
GLA5_W = 5 * GLA_W
SSD_W = SSD_CONV_DIM + SSD_INNER
SWA_W = SWA_QW + 2 * SWA_KW


def _in_kernel(x_ref, g_ref, w_ref, wa2_ref, ba_ref, gla_ref, ssd_ref, dt_ref, swa_ref):
    xb = _rms(x_ref[...], g_ref[...]).astype(bf16)
    mm = lambda lo, hi: jnp.dot(xb, w_ref[:, lo:hi], preferred_element_type=f32)
    gla_ref[:, 0:4 * GLA_W] = mm(_P_GLA, _P_GA)
    ga = mm(_P_GA, _P_XBC)
    gla_ref[:, 4 * GLA_W:] = _log_sigmoid(_dot(ga, wa2_ref[...]) + ba_ref[...]) / GLA_TAU
    ssd_ref[...] = mm(_P_XBC, _P_DT)
    dt_ref[...] = mm(_P_DT, _P_SWA)
    swa_ref[...] = mm(_P_SWA, _P_END)


def _in_proj(x2, g, w_pack, wa2_pad, ba):
    t, d = x2.shape
    tm = min(TOKEN_TILE, t)
    row = lambda w: pl.BlockSpec((tm, w), lambda i: (i, 0))
    return pl.pallas_call(
        _in_kernel,
        grid=(t // tm,),
        in_specs=[row(d), _resident((1, d)), _resident(w_pack.shape), _resident(wa2_pad.shape),
                  _resident((1, GLA_W))],
        out_specs=[row(GLA5_W), row(SSD_W), row(V7X_LANES), row(SWA_W)],
        out_shape=[jax.ShapeDtypeStruct((t, GLA5_W), f32), jax.ShapeDtypeStruct((t, SSD_W), f32),
                   jax.ShapeDtypeStruct((t, V7X_LANES), f32), jax.ShapeDtypeStruct((t, SWA_W), f32)],
        compiler_params=_params("parallel"),
        name="in_proj",
    )(x2, g, w_pack, wa2_pad, ba)


def _gla_kernel(*refs, chunk, has_init):
    if has_init:
        q_ref, k_ref, v_ref, r_ref, la_ref, gn_ref, s0_ref, o_ref, sout_ref, s_scr = refs
    else:
        q_ref, k_ref, v_ref, r_ref, la_ref, gn_ref, o_ref, sout_ref, s_scr = refs
    i = pl.program_id(1)

    @pl.when(i == 0)
    def _():
        s_scr[...] = s0_ref[...] if has_init else jnp.zeros(s_scr.shape, f32)

    nb, tl = q_ref.shape[0], q_ref.shape[1]
    causal = _tri_incl(chunk)
    tri = causal.astype(bf16)

    for bb in range(nb):
        for c in range(tl // chunk):
            rows = slice(c * chunk, (c + 1) * chunk)
            bc = _dot_exact_lhs(tri, la_ref[bb, rows, :])
            b_end = bc[chunk - 1:chunk, :]
            k = k_ref[bb, rows, :]
            v = v_ref[bb, rows, :]
            qt = q_ref[bb, rows, :] * (GLA_DK ** -0.5) * jnp.exp(bc)
            kt = k * jnp.exp(-bc)
            kh = k * jnp.exp(b_end - bc)
            e_end = jnp.exp(b_end)
            for h in range(GLA_HEADS):
                sl = slice(h * GLA_DK, (h + 1) * GLA_DK)
                s_old = s_scr[bb, h]
                scores = jnp.where(causal, _dot_nt(qt[:, sl], kt[:, sl]), 0.0)
                o = _dot(scores, v[:, sl]) + _dot(qt[:, sl], s_old)
                s_scr[bb, h] = _row_to_col(e_end[:, sl]) * s_old + _dot_tn(kh[:, sl], v[:, sl])
                o_ref[bb, rows, sl] = _rms(o, gn_ref[...]) * _silu(r_ref[bb, rows, sl])

    @pl.when(i == pl.num_programs(1) - 1)
    def _():
        sout_ref[...] = s_scr[...]


def _gla(gla5, gn, s0_all, layer):
    b, l, _ = gla5.shape
    tl = min(GLA_TILE, l)
    chunk = min(GLA_CHUNK, l)
    nb = _seqs_per_step(b, l)
    col = lambda j: pl.BlockSpec((nb, tl, GLA_W), lambda bi, i: (bi, i, j))
    st = pl.BlockSpec((nb, GLA_HEADS, GLA_DK, GLA_DV), lambda bi, i: (bi, 0, 0, 0))
    in_specs = [col(0), col(1), col(2), col(3), col(4), _resident((1, GLA_DV))]
    args = [gla5] * 5 + [gn]
    if s0_all is not None:
        in_specs.append(pl.BlockSpec((None, nb, GLA_HEADS, GLA_DK, GLA_DV), lambda bi, i: (layer, bi, 0, 0, 0)))
        args.append(s0_all)
    return pl.pallas_call(
        functools.partial(_gla_kernel, chunk=chunk, has_init=s0_all is not None),
        grid=(b // nb, l // tl),
        in_specs=in_specs,
        out_specs=[col(0), st],
        out_shape=[jax.ShapeDtypeStruct((b, l, GLA_W), f32),
                   jax.ShapeDtypeStruct((b, GLA_HEADS, GLA_DK, GLA_DV), f32)],
        scratch_shapes=[pltpu.VMEM((nb, GLA_HEADS, GLA_DK, GLA_DV), f32)],
        compiler_params=_params("parallel", "arbitrary"),
        name="gla",
    )(*args)


_TAIL_ROWS = V7X_SUBLANES


def _ssd_kernel(*refs, has_init):
    if has_init:
        (xbc_ref, z_ref, dt_ref, cw_ref, cb_ref, dtb_ref, alog_ref, d_ref, nrm_ref, h0_ref, cbuf_ref,
         y_ref, hout_ref, cout_ref, h_scr, tail_scr) = refs
    else:
        (xbc_ref, z_ref, dt_ref, cw_ref, cb_ref, dtb_ref, alog_ref, d_ref, nrm_ref,
         y_ref, hout_ref, cout_ref, h_scr, tail_scr) = refs
    i = pl.program_id(1)
    last = pl.num_programs(1) - 1
    nk = SSD_CONV - 1

    @pl.when(i == 0)
    def _():
        h_scr[...] = h0_ref[...] if has_init else jnp.zeros(h_scr.shape, f32)
        tail_scr[...] = jnp.zeros(tail_scr.shape, f32)
        if has_init:
            tail_scr[:, _TAIL_ROWS - nk:, :] = cbuf_ref[...]

    nb, tl = xbc_ref.shape[0], xbc_ref.shape[1]
    a = -jnp.exp(alog_ref[...])
    causal = _tri_incl(tl)
    tri = causal.astype(bf16)
    gw = SSD_INNER // SSD_GROUPS
    for bb in range(nb):
        raw = xbc_ref[bb]
        xp = jnp.concatenate([tail_scr[bb], raw], axis=0)
        conv = cb_ref[...]
        for j in range(SSD_CONV):
            o = _TAIL_ROWS - nk + j
            conv = conv + xp[o:o + tl] * cw_ref[j:j + 1, :]
        xc = _silu(conv)
        if tl >= _TAIL_ROWS:
            tail_scr[bb] = raw[tl - _TAIL_ROWS:, :]

        @pl.when(i == last)
        def _():
            cout_ref[bb] = xp[_TAIL_ROWS + tl - nk:, :]

        dt = _softplus(dt_ref[bb] + dtb_ref[...])
        cum = _dot_exact_lhs(tri, dt * a)
        cum_row = cum.T
        dt_row = dt.T
        ys = []
        for g in range(SSD_GROUPS):
            bm = xc[:, SSD_INNER + g * SSD_STATE:SSD_INNER + (g + 1) * SSD_STATE]
            cm = xc[:, SSD_INNER + (SSD_GROUPS + g) * SSD_STATE:SSD_INNER + (SSD_GROUPS + g + 1) * SSD_STATE]
            cb = _dot_nt(cm, bm)
            for r in range(SSD_HPG):
                h = g * SSD_HPG + r
                xh = xc[:, h * SSD_HEADDIM:(h + 1) * SSD_HEADDIM]
                cum_c = cum[:, h:h + 1]
                lmat = jnp.exp(jnp.where(causal, cum_c - cum_row[h:h + 1, :], -jnp.inf))
                w = cb * lmat * dt_row[h:h + 1, :]
                h_old = h_scr[bb, h]
                yh = _dot(w, xh) + _dot_nt(cm, h_old) * jnp.exp(cum_c)
                c_end = cum[tl - 1:tl, h:h + 1]
                wcol = jnp.exp(c_end - cum_c) * dt[:, h:h + 1]
                h_scr[bb, h] = jnp.exp(c_end) * h_old + _dot_tn(xh * wcol, bm)
                ys.append(yh + d_ref[:, h:h + 1] * xh)
        y = jnp.concatenate(ys, axis=1) * _silu(z_ref[bb])
        for g in range(SSD_GROUPS):
            sl = slice(g * gw, (g + 1) * gw)
            y_ref[bb, :, sl] = _rms(y[:, sl], nrm_ref[:, sl])

    @pl.when(i == last)
    def _():
        hout_ref[...] = h_scr[...]


def _ssd(ssd, dtp, lw, h0_all, cbuf_all, layer):
    b, l, _ = ssd.shape
    tl = min(SSD_TILE, l)
    nk = SSD_CONV - 1
    nb = _seqs_per_step(b, l)
    st = pl.BlockSpec((nb, SSD_HEADS, SSD_HEADDIM, SSD_STATE), lambda bi, i: (bi, 0, 0, 0))
    cv = pl.BlockSpec((nb, nk, SSD_CONV_DIM), lambda bi, i: (bi, 0, 0))
    in_specs = [pl.BlockSpec((nb, tl, SSD_CONV_DIM), lambda bi, i: (bi, i, 0)),
                pl.BlockSpec((nb, tl, SSD_INNER), lambda bi, i: (bi, i, SSD_CONV_DIM // SSD_INNER)),
                pl.BlockSpec((nb, tl, V7X_LANES), lambda bi, i: (bi, i, 0)),
                _resident((SSD_CONV, SSD_CONV_DIM)), _resident((1, SSD_CONV_DIM)),
                _resident((1, V7X_LANES)), _resident((1, V7X_LANES)), _resident((1, V7X_LANES)),
                _resident((1, SSD_INNER))]
    args = [ssd, ssd, dtp, lw["ssd_conv_w"], lw["ssd_conv_b"], lw["ssd_dt_bias"], lw["ssd_a_log"],
            lw["ssd_d"], lw["ssd_norm"]]
    if h0_all is not None:
        in_specs += [pl.BlockSpec((None, nb, SSD_HEADS, SSD_HEADDIM, SSD_STATE),
                                  lambda bi, i: (layer, bi, 0, 0, 0)),
                     pl.BlockSpec((None, nb, nk, SSD_CONV_DIM), lambda bi, i: (layer, bi, 0, 0))]
        args += [h0_all, cbuf_all]
    return pl.pallas_call(
        functools.partial(_ssd_kernel, has_init=h0_all is not None),
        grid=(b // nb, l // tl),
        in_specs=in_specs,
        out_specs=[pl.BlockSpec((nb, tl, SSD_INNER), lambda bi, i: (bi, i, 0)), st, cv],
        out_shape=[jax.ShapeDtypeStruct((b, l, SSD_INNER), f32),
                   jax.ShapeDtypeStruct((b, SSD_HEADS, SSD_HEADDIM, SSD_STATE), f32),
                   jax.ShapeDtypeStruct((b, nk, SSD_CONV_DIM), f32)],
        scratch_shapes=[pltpu.VMEM((nb, SSD_HEADS, SSD_HEADDIM, SSD_STATE), f32),
                        pltpu.VMEM((nb, _TAIL_ROWS, SSD_CONV_DIM), f32)],
        compiler_params=_params("parallel", "arbitrary"),
        name="ssd",
    )(*args)


def _sink_softmax(parts, sink):
    m = sink
    for s in parts:
        m = jnp.maximum(m, jnp.max(s, axis=-1, keepdims=True))
    ps = [jnp.exp(s - m) for s in parts]
    den = jnp.exp(sink - m)
    for p in ps:
        den = den + jnp.sum(p, axis=-1, keepdims=True)
    return [p / den for p in ps]


def _stack_rep_heads(q, g):
    h0 = g * SWA_REP
    return jnp.concatenate([q[:, (h0 + r) * SWA_HEAD_DIM:(h0 + r + 1) * SWA_HEAD_DIM]
                            for r in range(SWA_REP)], axis=0)


def _rep_sinks(sink_ref, g, l):
    rep = lax.broadcasted_iota(jnp.int32, (SWA_REP * l, 1), 0) // l
    sink = jnp.zeros((SWA_REP * l, 1), f32)
    for r in range(SWA_REP):
        sink = jnp.where(rep == r, sink_ref[g * SWA_REP + r], sink)
    return sink


def _swa_prompt_kernel(sink_ref, q_ref, kv_ref, kvp_ref, o_ref, *, nwin):
    i = pl.program_id(1)
    w = WINDOW
    rows = SWA_REP * w
    t = lax.broadcasted_iota(jnp.int32, (rows, 2 * w), 0) % w
    j = lax.broadcasted_iota(jnp.int32, (rows, 2 * w), 1)
    rel = t - j + w
    kvp = jnp.where(i > 0, kvp_ref[0], 0.0)
    for win in range(nwin):
        tok = slice(win * w, (win + 1) * w)
        q = q_ref[0, tok, :]
        kv = kv_ref[0, tok, :]
        kpos = (i * nwin + win) * w + j - w
        mask = (rel >= 0) & (rel < w) & (kpos >= 0)
        for g in range(SWA_KV_HEADS):
            ks = slice(g * SWA_HEAD_DIM, (g + 1) * SWA_HEAD_DIM)
            vs = slice(SWA_KW + g * SWA_HEAD_DIM, SWA_KW + (g + 1) * SWA_HEAD_DIM)
            kc = jnp.concatenate([kvp[:, ks], kv[:, ks]], axis=0)
            vc = jnp.concatenate([kvp[:, vs], kv[:, vs]], axis=0)
            s = jnp.where(mask, _dot_nt(_stack_rep_heads(q, g), kc) * (SWA_HEAD_DIM ** -0.5), -jnp.inf)
            (p,) = _sink_softmax([s], _rep_sinks(sink_ref, g, w))
            o = _dot(p, vc)
            for r in range(SWA_REP):
                h = g * SWA_REP + r
                o_ref[0, tok, h * SWA_HEAD_DIM:(h + 1) * SWA_HEAD_DIM] = o[r * w:(r + 1) * w]
        kvp = kv


def _swa_prompt(swa, sinks):
    b, l, _ = swa.shape
    w = WINDOW
    nwin = SWA_WINDOWS_PER_STEP if (l // w) % SWA_WINDOWS_PER_STEP == 0 else 1
    tl = nwin * w
    kvb = SWA_QW // (2 * SWA_KW)
    return pl.pallas_call(
        functools.partial(_swa_prompt_kernel, nwin=nwin),
        grid=(b, l // tl),
        in_specs=[pl.BlockSpec(memory_space=pltpu.SMEM),
                  pl.BlockSpec((1, tl, SWA_QW), lambda bi, i: (bi, i, 0)),
                  pl.BlockSpec((1, tl, 2 * SWA_KW), lambda bi, i: (bi, i, kvb)),
                  pl.BlockSpec((1, w, 2 * SWA_KW), lambda bi, i: (bi, jnp.maximum(i * nwin - 1, 0), kvb))],
        out_specs=pl.BlockSpec((1, tl, SWA_QW), lambda bi, i: (bi, i, 0)),
        out_shape=jax.ShapeDtypeStruct((b, l, SWA_QW), f32),
        compiler_params=_params("parallel", "arbitrary"),
        name="swa_prompt",
    )(sinks, swa, swa, swa)


def _swa_sample_kernel(sink_ref, q_ref, kv_ref, ck_ref, cv_ref, o_ref, ko_ref, vo_ref):
    nb, l = q_ref.shape[0], q_ref.shape[1]
    nbuf = ck_ref.shape[1]
    rows = SWA_REP * l

    def mask(ncols, off):
        t = lax.broadcasted_iota(jnp.int32, (rows, ncols), 0) % l
        j = lax.broadcasted_iota(jnp.int32, (rows, ncols), 1) + off
        rel = t - j + nbuf
        return (rel >= 0) & (rel < WINDOW) & (PAST_LEN - nbuf + j >= 0)

    m_buf = mask(nbuf, 0)
    m_new = mask(l, nbuf)
    scale = SWA_HEAD_DIM ** -0.5
    for bb in range(nb):
        q = q_ref[bb]
        kv = kv_ref[bb]
        for g in range(SWA_KV_HEADS):
            kb = ck_ref[bb, :, g, :]
            vb = cv_ref[bb, :, g, :]
            kn = kv[:, g * SWA_HEAD_DIM:(g + 1) * SWA_HEAD_DIM]
            vn = kv[:, SWA_KW + g * SWA_HEAD_DIM:SWA_KW + (g + 1) * SWA_HEAD_DIM]
            q4 = _stack_rep_heads(q, g)
            s_buf = jnp.where(m_buf, _dot_nt(q4, kb) * scale, -jnp.inf)
            s_new = jnp.where(m_new, _dot_nt(q4, kn) * scale, -jnp.inf)
            p_buf, p_new = _sink_softmax([s_buf, s_new], _rep_sinks(sink_ref, g, l))
            o = _dot(p_buf, vb) + _dot(p_new, vn)
            for r in range(SWA_REP):
                h = g * SWA_REP + r
                o_ref[bb, :, h * SWA_HEAD_DIM:(h + 1) * SWA_HEAD_DIM] = o[r * l:(r + 1) * l]
            ko_ref[bb, 0:nbuf - l, g, :] = kb[l:]
            ko_ref[bb, nbuf - l:, g, :] = kn
            vo_ref[bb, 0:nbuf - l, g, :] = vb[l:]
            vo_ref[bb, nbuf - l:, g, :] = vn


def _swa_sample(swa, ck_all, cv_all, sinks, layer):
    b, l, _ = swa.shape
    nbuf = ck_all.shape[2]
    nb = _seqs_per_step(b, l)
    kvb = SWA_QW // (2 * SWA_KW)
    cache_in = pl.BlockSpec((None, nb, nbuf, SWA_KV_HEADS, SWA_HEAD_DIM), lambda bi: (layer, bi, 0, 0, 0))
    cache_out = pl.BlockSpec((nb, nbuf, SWA_KV_HEADS, SWA_HEAD_DIM), lambda bi: (bi, 0, 0, 0))
    cache_shape = jax.ShapeDtypeStruct((b, nbuf, SWA_KV_HEADS, SWA_HEAD_DIM), f32)
    return pl.pallas_call(
        _swa_sample_kernel,
        grid=(b // nb,),
        in_specs=[pl.BlockSpec(memory_space=pltpu.SMEM),
                  pl.BlockSpec((nb, l, SWA_QW), lambda bi: (bi, 0, 0)),
                  pl.BlockSpec((nb, l, 2 * SWA_KW), lambda bi: (bi, 0, kvb)),
                  cache_in, cache_in],
        out_specs=[pl.BlockSpec((nb, l, SWA_QW), lambda bi: (bi, 0, 0)), cache_out, cache_out],
        out_shape=[jax.ShapeDtypeStruct((b, l, SWA_QW), f32), cache_shape, cache_shape],
        compiler_params=_params("parallel"),
        name="swa_sample",
    )(sinks, swa, swa, ck_all, cv_all)


def _merge_kernel(x_ref, g_ref, og_ref, os_ref, oa_ref, wgate_ref, wbg_ref, wbs_ref, wba_ref, wout_ref, y_ref):
    x = x_ref[...]
    d = x.shape[1]
    xb = _rms(x, g_ref[...]).astype(bf16)
    merged = None
    for j, (br_ref, w_ref) in enumerate(((og_ref, wbg_ref), (os_ref, wbs_ref), (oa_ref, wba_ref))):
        gate = _sigmoid(jnp.dot(xb, wgate_ref[:, j * d:(j + 1) * d], preferred_element_type=f32))
        term = gate * _dot(br_ref[...], w_ref[...])
        merged = term if merged is None else merged + term
    y_ref[...] = x + _dot(merged, wout_ref[...])


def _merge(x2, g, o_gla, y_ssd, o_swa, lw):
    t, d = x2.shape
    tm = min(TOKEN_TILE, t)
    row = lambda w: pl.BlockSpec((tm, w), lambda i: (i, 0))
    ws = [lw["w_gate"], lw["w_br_gla"], lw["w_br_ssd"], lw["w_br_swa"], lw["w_out"]]
    return pl.pallas_call(
        _merge_kernel,
        grid=(t // tm,),
        in_specs=[row(d), _resident((1, d)), row(GLA_W), row(SSD_INNER), row(SWA_QW)]
                 + [_resident(w.shape) for w in ws],
        out_specs=row(d),
        out_shape=jax.ShapeDtypeStruct((t, d), f32),
        compiler_params=_params("parallel"),
        name="merge",
    )(x2, g, o_gla, y_ssd, o_swa, *ws)


def _rms_matmul_kernel(x_ref, g_ref, w_ref, y_ref):
    y_ref[...] = jnp.dot(_rms(x_ref[...], g_ref[...]).astype(bf16), w_ref[...], preferred_element_type=f32)


def _rms_matmul(x2, g, w):
    t, d = x2.shape
    n = w.shape[1]
    tm = min(TOKEN_TILE, t)
    return pl.pallas_call(
        _rms_matmul_kernel,
        grid=(t // tm,),
        in_specs=[pl.BlockSpec((tm, d), lambda i: (i, 0)), _resident((1, d)), _resident(w.shape)],
        out_specs=pl.BlockSpec((tm, n), lambda i: (i, 0)),
        out_shape=jax.ShapeDtypeStruct((t, n), f32),
        compiler_params=_params("parallel"),
        name="rms_matmul",
    )(x2, g, w)


def _matmul_res_kernel(a_ref, w_ref, x_ref, y_ref):
    y_ref[...] = x_ref[...] + _dot(a_ref[...], w_ref[...])


def _matmul_res(a2, w, x2):
    t, d = x2.shape
    tm = min(TOKEN_TILE, t)
    return pl.pallas_call(
        _matmul_res_kernel,
        grid=(t // tm,),
        in_specs=[pl.BlockSpec((tm, a2.shape[1]), lambda i: (i, 0)), _resident(w.shape),
                  pl.BlockSpec((tm, d), lambda i: (i, 0))],
        out_specs=pl.BlockSpec((tm, d), lambda i: (i, 0)),
        out_shape=jax.ShapeDtypeStruct((t, d), f32),
        compiler_params=_params("parallel"),
        name="matmul_res",
    )(a2, w, x2)


def _mem_fused_kernel(x_ref, g_ref, wq_ref, k_ref, v_ref, wo_ref, y_ref):
    x = x_ref[...]
    q = jnp.dot(_rms(x, g_ref[...]).astype(bf16), wq_ref[...], preferred_element_type=f32)
    hd = q.shape[1] // MEM_HEADS
    os = []
    for h in range(MEM_HEADS):
        sl = slice(h * hd, (h + 1) * hd)
        p = _softmax_rows(_dot_nt(q[:, sl], k_ref[0, :, sl]) * (hd ** -0.5))
        os.append(_dot(p, v_ref[0, :, sl]))
    y_ref[...] = x + _dot(jnp.concatenate(os, axis=1), wo_ref[...])


def _mem_fused(x2, b, g, wq, wo, mk3, mv3):
    t, d = x2.shape
    l = t // b
    m = mk3.shape[1]
    tq = min(TOKEN_TILE, l)
    nq = l // tq
    row = pl.BlockSpec((tq, d), lambda bi, i: (bi * nq + i, 0))
    kv = pl.BlockSpec((1, m, d), lambda bi, i: (bi, 0, 0))
    return pl.pallas_call(
        _mem_fused_kernel,
        grid=(b, nq),
        in_specs=[row, _resident((1, d)), _resident(wq.shape), kv, kv, _resident(wo.shape)],
        out_specs=row,
        out_shape=jax.ShapeDtypeStruct((t, d), f32),
        compiler_params=_params("parallel", "arbitrary"),
        name="mem_fused",
    )(x2, g, wq, mk3, mv3, wo)


def _mem_cache_attn_kernel(q_ref, k_ref, v_ref, o_ref, kh_scr, vh_scr):
    nb = q_ref.shape[0]
    hd = k_ref.shape[3]
    for bb in range(nb):
        for h in range(MEM_HEADS):
            sl = slice(h * hd, (h + 1) * hd)
            kh_scr[...] = k_ref[bb, :, h, :]
            vh_scr[...] = v_ref[bb, :, h, :]
            p = _softmax_rows(_dot_nt(q_ref[bb, :, sl], kh_scr[...]) * (hd ** -0.5))
            o_ref[bb, :, sl] = _dot(p, vh_scr[...])


def _mem_cache_attn(q3, ck_all, cv_all, layer):
    b, l, d = q3.shape
    _, _, m, nh, hd = ck_all.shape
    nb = MEM_CACHE_BATCH if b % MEM_CACHE_BATCH == 0 else 1
    cache = pl.BlockSpec((None, nb, m, nh, hd), lambda bi: (layer, bi, 0, 0, 0))
    return pl.pallas_call(
        _mem_cache_attn_kernel,
        grid=(b // nb,),
        in_specs=[pl.BlockSpec((nb, l, d), lambda bi: (bi, 0, 0)), cache, cache],
        out_specs=pl.BlockSpec((nb, l, d), lambda bi: (bi, 0, 0)),
        out_shape=jax.ShapeDtypeStruct((b, l, d), f32),
        scratch_shapes=[pltpu.VMEM((m, hd), f32), pltpu.VMEM((m, hd), f32)],
        compiler_params=_params("parallel"),
        name="mem_cache_attn",
    )(q3, ck_all, cv_all)


def _moe_kernel(x_ref, g_ref, wrh_ref, wrl_ref, br_ref, wg_ref, wu_ref, wd_ref, gf_ref, o_ref,
                xb_scr, comb_scr, *, final_norm):
    grp = pl.program_id(1)
    tm = x_ref.shape[0]
    lane = lax.broadcasted_iota(jnp.int32, (tm, V7X_LANES), 1)
    lane_f = lane.astype(f32)
    first = lambda hit: jnp.min(jnp.where(hit, lane_f, float(V7X_LANES)), axis=-1, keepdims=True)

    @pl.when(grp == 0)
    def _():
        xn = _rms(x_ref[...], g_ref[...])
        xh = xn.astype(bf16)
        xl = (xn - xh.astype(f32)).astype(bf16)
        xb_scr[...] = xh
        d = lambda a, w: jnp.dot(a, w[...], preferred_element_type=f32)
        logits = d(xh, wrh_ref) + (d(xl, wrh_ref) + d(xh, wrl_ref)) + br_ref[...]
        lg = jnp.where(lane < N_GROUPS, logits[:, :V7X_LANES], -jnp.inf)
        mg = jnp.max(lg, axis=-1, keepdims=True)
        gi = first(lg == mg)
        pg = 1.0 / jnp.sum(jnp.exp(lg - mg), axis=-1, keepdims=True)
        in_grp = ((lane // EXPERTS_PER_GROUP).astype(f32) == gi) & (lane < N_EXPERTS)
        le = jnp.where(in_grp, logits[:, V7X_LANES:], -jnp.inf)
        m1 = jnp.max(le, axis=-1, keepdims=True)
        i1 = first(le == m1)
        z = jnp.sum(jnp.exp(le - m1), axis=-1, keepdims=True)
        le2 = jnp.where(lane_f == i1, -jnp.inf, le)
        m2 = jnp.max(le2, axis=-1, keepdims=True)
        i2 = first(le2 == m2)
        v1 = 1.0 / z
        v2 = jnp.exp(m2 - m1) / z
        tot = v1 + v2
        comb_scr[...] = jnp.where(lane_f == i1, pg * v1 / tot, jnp.where(lane_f == i2, pg * v2 / tot, 0.0))
        o_ref[...] = jnp.zeros(o_ref.shape, f32)

    xb = xb_scr[...]
    comb = comb_scr[...]
    acc = jnp.zeros(o_ref.shape, f32)
    for e in range(EXPERTS_PER_GROUP):
        ce = jnp.sum(jnp.where(lane == grp * EXPERTS_PER_GROUP + e, comb, 0.0), axis=-1, keepdims=True)
        hg = jnp.dot(xb, wg_ref[e], preferred_element_type=f32)
        hu = jnp.dot(xb, wu_ref[e], preferred_element_type=f32)
        acc = acc + _dot(_silu(hg) * hu * ce, wd_ref[e])
    o_ref[...] += acc

    @pl.when(grp == pl.num_programs(1) - 1)
    def _():
        y = x_ref[...] + o_ref[...]
        o_ref[...] = _rms(y, gf_ref[...]) if final_norm else y


def _moe(x2, g, lw, g_final, final_norm):
    t, d = x2.shape
    tm = min(MOE_TILE, t)
    ff = lw["moe_w_gate"].shape[2]
    e = EXPERTS_PER_GROUP
    return pl.pallas_call(
        functools.partial(_moe_kernel, final_norm=final_norm),
        grid=(t // tm, N_GROUPS),
        in_specs=[pl.BlockSpec((tm, d), lambda i, j: (i, 0)), _resident((1, d)),
                  _resident(lw["moe_wr_hi"].shape), _resident(lw["moe_wr_lo"].shape),
                  _resident((1, 2 * V7X_LANES)),
                  pl.BlockSpec((e, d, ff), lambda i, j: (j, 0, 0)),
                  pl.BlockSpec((e, d, ff), lambda i, j: (j, 0, 0)),
                  pl.BlockSpec((e, ff, d), lambda i, j: (j, 0, 0)),
                  _resident((1, d))],
        out_specs=pl.BlockSpec((tm, d), lambda i, j: (i, 0)),
        out_shape=jax.ShapeDtypeStruct((t, d), f32),
        scratch_shapes=[pltpu.VMEM((tm, d), bf16), pltpu.VMEM((tm, V7X_LANES), f32)],
        compiler_params=_params("parallel", "arbitrary"),
        name="moe",
    )(x2, g, lw["moe_wr_hi"], lw["moe_wr_lo"], lw["moe_br"], lw["moe_w_gate"], lw["moe_w_up"],
      lw["moe_w_down"], g_final)


def _pad_cols(w, n):
    return jnp.pad(w, ((0, 0), (0, n - w.shape[1])))


def _pad_lanes(v):
    return _pad_cols(v.reshape(1, -1), V7X_LANES)


def _layer_weights(p, l):
    w_in = p["w_in"][l]
    sizes = (GLA_W, GLA_W, GLA_W, GLA_W, GLA_LOWRANK, SSD_INNER, SSD_CONV_DIM, SSD_HEADS,
             SWA_QW, SWA_KW, SWA_KW)
    offs, o = [], 0
    for n in sizes:
        offs.append(o)
        o += n
    seg = lambda idx: w_in[:, offs[idx]:offs[idx] + sizes[idx]]
    gq, gk, gv, gr, ga, sz, sxbc, sdt, aq, ak, av = (seg(i) for i in range(len(sizes)))
    w_pack = jnp.concatenate([gq, gk, gv, gr, _pad_cols(ga, V7X_LANES), sxbc, sz,
                              _pad_cols(sdt, V7X_LANES), aq, ak, av], axis=1).astype(bf16)
    d = w_in.shape[0]
    router = jnp.concatenate([_pad_cols(p["moe_w_group"][l], V7X_LANES),
                              _pad_cols(p["moe_w_expert"][l], V7X_LANES)], axis=1)
    router_hi = router.astype(bf16)
    return dict(
        w_pack=w_pack,
        w_gate=w_in[:, o:o + N_BRANCHES * d].astype(bf16),
        wa2=jnp.pad(p["gla_w_a2"][l], ((0, V7X_LANES - GLA_LOWRANK), (0, 0))).astype(bf16),
        gla_b_a=p["gla_b_a"][l].reshape(1, -1),
        gla_norm=p["gla_norm"][l].reshape(1, -1),
        ssd_conv_w=p["ssd_conv_w"][l],
        ssd_conv_b=p["ssd_conv_b"][l].reshape(1, -1),
        ssd_dt_bias=_pad_lanes(p["ssd_dt_bias"][l]),
        ssd_a_log=_pad_lanes(p["ssd_a_log"][l]),
        ssd_d=_pad_lanes(p["ssd_d"][l]),
        ssd_norm=p["ssd_norm"][l].reshape(1, -1),
        swa_sinks=p["swa_sinks"][l],
        w_br_gla=p["w_br_gla"][l].astype(bf16),
        w_br_ssd=p["w_br_ssd"][l].astype(bf16),
        w_br_swa=p["w_br_swa"][l].astype(bf16),
        w_out=p["w_out"][l].astype(bf16),
        norm_mix=p["norm_mix"][l].reshape(1, -1),
        norm_mem=p["norm_mem"][l].reshape(1, -1),
        norm_memkv=p["norm_memkv"][l].reshape(1, -1),
        norm_ffn=p["norm_ffn"][l].reshape(1, -1),
        mem_wq=p["mem_wq"][l].astype(bf16),
        mem_wk=p["mem_wk"][l].astype(bf16),
        mem_wv=p["mem_wv"][l].astype(bf16),
        mem_wo=p["mem_wo"][l].astype(bf16),
        moe_wr_hi=router_hi,
        moe_wr_lo=(router - router_hi.astype(f32)).astype(bf16),
        moe_br=jnp.concatenate([_pad_lanes(p["moe_b_group"][l]), _pad_lanes(p["moe_b_expert"][l])], axis=1),
        moe_w_gate=p["moe_w_gate"][l].astype(bf16),
        moe_w_up=p["moe_w_up"][l].astype(bf16),
        moe_w_down=p["moe_w_down"][l].astype(bf16),
    )


def _mixer(h3, lw, layer, state):
    b, l, d = h3.shape
    x2 = h3.reshape(b * l, d)
    gla_s0, ssd_h0, conv_buf, swa_kbuf, swa_vbuf = state if state is not None else (None,) * 5
    gla5, ssd, dtp, swa = _in_proj(x2, lw["norm_mix"], lw["w_pack"], lw["wa2"], lw["gla_b_a"])
    o_gla, gla_s = _gla(gla5.reshape(b, l, GLA5_W), lw["gla_norm"], gla_s0, layer)
    y_ssd, ssd_h, conv_s = _ssd(ssd.reshape(b, l, SSD_W), dtp.reshape(b, l, V7X_LANES), lw, ssd_h0, conv_buf,
                                layer)
    swa3 = swa.reshape(b, l, SWA_W)
    if state is None:
        o_swa = _swa_prompt(swa3, lw["swa_sinks"])
        nkeep = min(WINDOW, l)
        kv_shape = (b, nkeep, SWA_KV_HEADS, SWA_HEAD_DIM)
        swa_k = swa3[:, l - nkeep:, SWA_QW:SWA_QW + SWA_KW].reshape(kv_shape)
        swa_v = swa3[:, l - nkeep:, SWA_QW + SWA_KW:].reshape(kv_shape)
    else:
        o_swa, swa_k, swa_v = _swa_sample(swa3, swa_kbuf, swa_vbuf, lw["swa_sinks"], layer)
    y2 = _merge(x2, lw["norm_mix"], o_gla.reshape(b * l, GLA_W), y_ssd.reshape(b * l, SSD_INNER),
                o_swa.reshape(b * l, SWA_QW), lw)
    return y2, gla_s, ssd_h, conv_s, swa_k, swa_v


def kernel(x_prompt, x_sample, mem_prompt, cache_mem_k, cache_mem_v, cache_swa_k, cache_swa_v, state_gla, state_ssd, state_conv, norm_mix, w_in, gla_w_a2, gla_b_a, gla_norm, ssd_conv_w, ssd_conv_b, ssd_dt_bias, ssd_a_log, ssd_d, ssd_norm, swa_sinks, w_br_gla, w_br_ssd, w_br_swa, w_out, norm_mem, norm_memkv, mem_wq, mem_wk, mem_wv, mem_wo, norm_ffn, moe_w_group, moe_b_group, moe_w_expert, moe_b_expert, moe_w_gate, moe_w_up, moe_w_down, norm_final):
    p = dict(norm_mix=norm_mix, w_in=w_in, gla_w_a2=gla_w_a2, gla_b_a=gla_b_a, gla_norm=gla_norm,
             ssd_conv_w=ssd_conv_w, ssd_conv_b=ssd_conv_b, ssd_dt_bias=ssd_dt_bias, ssd_a_log=ssd_a_log,
             ssd_d=ssd_d, ssd_norm=ssd_norm, swa_sinks=swa_sinks, w_br_gla=w_br_gla, w_br_ssd=w_br_ssd,
             w_br_swa=w_br_swa, w_out=w_out, norm_mem=norm_mem, norm_memkv=norm_memkv, mem_wq=mem_wq,
             mem_wk=mem_wk, mem_wv=mem_wv, mem_wo=mem_wo, norm_ffn=norm_ffn, moe_w_group=moe_w_group,
             moe_b_group=moe_b_group, moe_w_expert=moe_w_expert, moe_b_expert=moe_b_expert,
             moe_w_gate=moe_w_gate, moe_w_up=moe_w_up, moe_w_down=moe_w_down)
    depth = w_in.shape[0]
    bp, lp, d = x_prompt.shape
    bs, ls, _ = x_sample.shape
    mlen = mem_prompt.shape[1]
    g_final = norm_final.reshape(1, -1)
    mem2 = mem_prompt.reshape(bp * mlen, d)
    sample_state = (state_gla, state_ssd, state_conv, cache_swa_k, cache_swa_v)
    hp = x_prompt
    hs = x_sample
    outs = [[] for _ in range(12)]
    for l in range(depth):
        lw = _layer_weights(p, l)
        last = l == depth - 1
        hp2, g_s, h_s, c_s, k_s, v_s = _mixer(hp, lw, l, None)
        mk = _rms_matmul(mem2, lw["norm_memkv"], lw["mem_wk"])
        mv = _rms_matmul(mem2, lw["norm_memkv"], lw["mem_wv"])
        hp2 = _mem_fused(hp2, bp, lw["norm_mem"], lw["mem_wq"], lw["mem_wo"],
                         mk.reshape(bp, mlen, d), mv.reshape(bp, mlen, d))
        hp = _moe(hp2, lw["norm_ffn"], lw, g_final, last).reshape(bp, lp, d)
        mem_shape = (bp, mlen, MEM_HEADS, d // MEM_HEADS)
        for lst, val in zip(outs[:7], (mk.reshape(mem_shape), mv.reshape(mem_shape), k_s, v_s, g_s, h_s, c_s)):
            lst.append(val)
        hs2, g_s, h_s, c_s, k_s, v_s = _mixer(hs, lw, l, sample_state)
        q = _rms_matmul(hs2, lw["norm_mem"], lw["mem_wq"])
        o = _mem_cache_attn(q.reshape(bs, ls, d), cache_mem_k, cache_mem_v, l)
        hs2 = _matmul_res(o.reshape(bs * ls, d), lw["mem_wo"], hs2)
        hs = _moe(hs2, lw["norm_ffn"], lw, g_final, last).reshape(bs, ls, d)
        for lst, val in zip(outs[7:], (k_s, v_s, g_s, h_s, c_s)):
            lst.append(val)
    return (hp, hs) + tuple(jnp.stack(o) for o in outs)
```

```python
import functools

import jax
import jax.numpy as jnp
from jax import lax
from jax.experimental import pallas as pl
from jax.experimental.pallas import tpu as pltpu

f32 = jnp.float32
bf16 = jnp.bfloat16

GLA_HEADS = 4
GLA_DK = 128
GLA_DV = 128
GLA_LOWRANK = 16
GLA_TAU = 16.0
SSD_HEADS = 8
SSD_HEADDIM = 64
SSD_GROUPS = 2
SSD_HPG = SSD_HEADS // SSD_GROUPS
SSD_STATE = 128
SSD_CONV = 4
SSD_INNER = SSD_HEADS * SSD_HEADDIM
SSD_CONV_DIM = SSD_INNER + 2 * SSD_GROUPS * SSD_STATE
SWA_HEADS = 8
SWA_KV_HEADS = 2
SWA_REP = SWA_HEADS // SWA_KV_HEADS
SWA_HEAD_DIM = 64
WINDOW = 128
PAST_LEN = 16384
MEM_HEADS = 4
N_GROUPS = 4
EXPERTS_PER_GROUP = 4
N_EXPERTS = N_GROUPS * EXPERTS_PER_GROUP
N_BRANCHES = 3
EPS = 1e-6

GLA_W = GLA_HEADS * GLA_DK
SWA_QW = SWA_HEADS * SWA_HEAD_DIM
SWA_KW = SWA_KV_HEADS * SWA_HEAD_DIM

V7X_LANES = 128
V7X_SUBLANES = 8
V7X_VMEM_LIMIT_BYTES = 56 * 1024 * 1024

TOKEN_TILE = 512
MOE_TILE = 1024
GLA_TILE = 256
GLA_CHUNK = 64
SSD_TILE = 256
SWA_WINDOWS_PER_STEP = 2
SHORT_SEQ_BATCH = 8
MEM_CACHE_BATCH = 4


def _params(*sem):
    return pltpu.CompilerParams(dimension_semantics=sem, vmem_limit_bytes=V7X_VMEM_LIMIT_BYTES)


def _resident(shape):
    n = len(shape)
    return pl.BlockSpec(shape, lambda *_: (0,) * n, pipeline_mode=pl.Buffered(1))


def _dot(a, b):
    return jnp.dot(a.astype(bf16), b.astype(bf16), preferred_element_type=f32)


def _dot_nt(a, b):
    return lax.dot_general(a.astype(bf16), b.astype(bf16), (((1,), (1,)), ((), ())),
                           preferred_element_type=f32)


def _dot_tn(a, b):
    return lax.dot_general(a.astype(bf16), b.astype(bf16), (((0,), (0,)), ((), ())),
                           preferred_element_type=f32)


def _split3(x):
    hi = x.astype(bf16)
    r = x - hi.astype(f32)
    mid = r.astype(bf16)
    lo = (r - mid.astype(f32)).astype(bf16)
    return hi, mid, lo


def _dot_exact_lhs(m_bf16, x):
    hi, mid, lo = _split3(x)
    d = lambda v: jnp.dot(m_bf16, v, preferred_element_type=f32)
    return d(hi) + d(mid) + d(lo)


def _rms(x, g):
    return x * lax.rsqrt(jnp.mean(x * x, axis=-1, keepdims=True) + EPS) * g


def _sigmoid(x):
    return jax.nn.sigmoid(x)


def _silu(x):
    return x * jax.nn.sigmoid(x)


def _softplus(x):
    return jnp.maximum(x, 0.0) + jnp.log1p(jnp.exp(-jnp.abs(x)))


def _log_sigmoid(x):
    return -_softplus(-x)


def _row_to_col(row):
    n = row.shape[1]
    return jnp.broadcast_to(row, (V7X_SUBLANES, n)).T[:, 0:1]


def _tri_incl(n):
    r = lax.broadcasted_iota(jnp.int32, (n, n), 0)
    c = lax.broadcasted_iota(jnp.int32, (n, n), 1)
    return r >= c


def _softmax_rows(s):
    p = jnp.exp(s - jnp.max(s, axis=-1, keepdims=True))
    return p / jnp.sum(p, axis=-1, keepdims=True)


def _seqs_per_step(b, l):
    return SHORT_SEQ_BATCH if l < V7X_SUBLANES and b % SHORT_SEQ_BATCH == 0 else 1


_P_GLA = 0
_P_GA = 4 * GLA_W
_P_XBC = _P_GA + V7X_LANES
_P_Z = _P_XBC + SSD_CONV_DIM
_P_DT = _P_Z + SSD_INNER
_P_SWA = _P_DT + V7X_LANES
_P_END = _P_SWA + SWA_QW + 2 * SWA_KW
GLA5_W = 5 * GLA_W
SSD_W = SSD_CONV_DIM + SSD_INNER
SWA_W = SWA_QW + 2 * SWA_KW


def _in_kernel(x_ref, g_ref, w_ref, wa2_ref, ba_ref, gla_ref, ssd_ref, dt_ref, swa_ref):
    xb = _rms(x_ref[...], g_ref[...]).astype(bf16)
    mm = lambda lo, hi: jnp.dot(xb, w_ref[:, lo:hi], preferred_element_type=f32)
    gla_ref[:, 0:4 * GLA_W] = mm(_P_GLA, _P_GA)
    ga = mm(_P_GA, _P_XBC)
    gla_ref[:, 4 * GLA_W:] = _log_sigmoid(_dot(ga, wa2_ref[...]) + ba_ref[...]) / GLA_TAU
    ssd_ref[...] = mm(_P_XBC, _P_DT)
    dt_ref[...] = mm(_P_DT, _P_SWA)
    swa_ref[...] = mm(_P_SWA, _P_END)


def _in_proj(x2, g, w_pack, wa2_pad, ba):
    t, d = x2.shape
    tm = min(TOKEN_TILE, t)
    row = lambda w: pl.BlockSpec((tm, w), lambda i: (i, 0))
    return pl.pallas_call(
        _in_kernel,
        grid=(t // tm,),
        in_specs=[row(d), _resident((1, d)), _resident(w_pack.shape), _resident(wa2_pad.shape),
                  _resident((1, GLA_W))],
        out_specs=[row(GLA5_W), row(SSD_W), row(V7X_LANES), row(SWA_W)],
        out_shape=[jax.ShapeDtypeStruct((t, GLA5_W), f32), jax.ShapeDtypeStruct((t, SSD_W), f32),
                   jax.ShapeDtypeStruct((t, V7X_LANES), f32), jax.ShapeDtypeStruct((t, SWA_W), f32)],
        compiler_params=_params("parallel"),
        name="in_proj",
    )(x2, g, w_pack, wa2_pad, ba)


def _gla_kernel(*refs, chunk, has_init):
    if has_init:
        q_ref, k_ref, v_ref, r_ref, la_ref, gn_ref, s0_ref, o_ref, sout_ref, s_scr = refs
    else:
        q_ref, k_ref, v_ref, r_ref, la_ref, gn_ref, o_ref, sout_ref, s_scr = refs
    i = pl.program_id(1)

    @pl.when(i == 0)
    def _():
        s_scr[...] = s0_ref[...] if has_init else jnp.zeros(s_scr.shape, f32)

    nb, tl = q_ref.shape[0], q_ref.shape[1]
    causal = _tri_incl(chunk)
    tri = causal.astype(bf16)

    for bb in range(nb):
        for c in range(tl // chunk):
            rows = slice(c * chunk, (c + 1) * chunk)
            bc = _dot_exact_lhs(tri, la_ref[bb, rows, :])
            b_end = bc[chunk - 1:chunk, :]
            k = k_ref[bb, rows, :]
            v = v_ref[bb, rows, :]
            qt = q_ref[bb, rows, :] * (GLA_DK ** -0.5) * jnp.exp(bc)
            kt = k * jnp.exp(-bc)
            kh = k * jnp.exp(b_end - bc)
            e_end = jnp.exp(b_end)
            for h in range(GLA_HEADS):
                sl = slice(h * GLA_DK, (h + 1) * GLA_DK)
                s_old = s_scr[bb, h]
                scores = jnp.where(causal, _dot_nt(qt[:, sl], kt[:, sl]), 0.0)
                o = _dot(scores, v[:, sl]) + _dot(qt[:, sl], s_old)
                s_scr[bb, h] = _row_to_col(e_end[:, sl]) * s_old + _dot_tn(kh[:, sl], v[:, sl])
                o_ref[bb, rows, sl] = _rms(o, gn_ref[...]) * _silu(r_ref[bb, rows, sl])

    @pl.when(i == pl.num_programs(1) - 1)
    def _():
        sout_ref[...] = s_scr[...]


def _gla(gla5, gn, s0_all, layer):
    b, l, _ = gla5.shape
    tl = min(GLA_TILE, l)
    chunk = min(GLA_CHUNK, l)
    nb = _seqs_per_step(b, l)
    col = lambda j: pl.BlockSpec((nb, tl, GLA_W), lambda bi, i: (bi, i, j))
    st = pl.BlockSpec((nb, GLA_HEADS, GLA_DK, GLA_DV), lambda bi, i: (bi, 0, 0, 0))
    in_specs = [col(0), col(1), col(2), col(3), col(4), _resident((1, GLA_DV))]
    args = [gla5] * 5 + [gn]
    if s0_all is not None:
        in_specs.append(pl.BlockSpec((None, nb, GLA_HEADS, GLA_DK, GLA_DV), lambda bi, i: (layer, bi, 0, 0, 0)))
        args.append(s0_all)
    return pl.pallas_call(
        functools.partial(_gla_kernel, chunk=chunk, has_init=s0_all is not None),
        grid=(b // nb, l // tl),
        in_specs=in_specs,
        out_specs=[col(0), st],
        out_shape=[jax.ShapeDtypeStruct((b, l, GLA_W), f32),
                   jax.ShapeDtypeStruct((b, GLA_HEADS, GLA_DK, GLA_DV), f32)],
        scratch_shapes=[pltpu.VMEM((nb, GLA_HEADS, GLA_DK, GLA_DV), f32)],
        compiler_params=_params("parallel", "arbitrary"),
        name="gla",
    )(*args)


_TAIL_ROWS = V7X_SUBLANES


def _ssd_kernel(*refs, has_init):
    if has_init:
        (xbc_ref, z_ref, dt_ref, cw_ref, cb_ref, dtb_ref, alog_ref, d_ref, nrm_ref, h0_ref, cbuf_ref,
         y_ref, hout_ref, cout_ref, h_scr, tail_scr) = refs
    else:
        (xbc_ref, z_ref, dt_ref, cw_ref, cb_ref, dtb_ref, alog_ref, d_ref, nrm_ref,
         y_ref, hout_ref, cout_ref, h_scr, tail_scr) = refs
    i = pl.program_id(1)
    last = pl.num_programs(1) - 1
    nk = SSD_CONV - 1

    @pl.when(i == 0)
    def _():
        h_scr[...] = h0_ref[...] if has_init else jnp.zeros(h_scr.shape, f32)
        tail_scr[...] = jnp.zeros(tail_scr.shape, f32)
        if has_init:
            tail_scr[:, _TAIL_ROWS - nk:, :] = cbuf_ref[...]

    nb, tl = xbc_ref.shape[0], xbc_ref.shape[1]
    a = -jnp.exp(alog_ref[...])
    causal = _tri_incl(tl)
    tri = causal.astype(bf16)
    gw = SSD_INNER // SSD_GROUPS
    for bb in range(nb):
        raw = xbc_ref[bb]
        xp = jnp.concatenate([tail_scr[bb], raw], axis=0)
        conv = cb_ref[...]
        for j in range(SSD_CONV):
            o = _TAIL_ROWS - nk + j
            conv = conv + xp[o:o + tl] * cw_ref[j:j + 1, :]
        xc = _silu(conv)
        if tl >= _TAIL_ROWS:
            tail_scr[bb] = raw[tl - _TAIL_ROWS:, :]

        @pl.when(i == last)
        def _():
            cout_ref[bb] = xp[_TAIL_ROWS + tl - nk:, :]

        dt = _softplus(dt_ref[bb] + dtb_ref[...])
        cum = _dot_exact_lhs(tri, dt * a)
        cum_row = cum.T
        dt_row = dt.T
        ys = []
        for g in range(SSD_GROUPS):
            bm = xc[:, SSD_INNER + g * SSD_STATE:SSD_INNER + (g + 1) * SSD_STATE]
            cm = xc[:, SSD_INNER + (SSD_GROUPS + g) * SSD_STATE:SSD_INNER + (SSD_GROUPS + g + 1) * SSD_STATE]
            cb = _dot_nt(cm, bm)
            h_grp = h_scr[bb, g * SSD_HPG:(g + 1) * SSD_HPG].reshape(SSD_HPG * SSD_HEADDIM, SSD_STATE)
            y_state = _dot_nt(cm, h_grp)
            xws = []
            for r in range(SSD_HPG):
                h = g * SSD_HPG + r
                hs = slice(r * SSD_HEADDIM, (r + 1) * SSD_HEADDIM)
                xh = xc[:, h * SSD_HEADDIM:(h + 1) * SSD_HEADDIM]
                cum_c = cum[:, h:h + 1]
                lmat = jnp.exp(jnp.where(causal, cum_c - cum_row[h:h + 1, :], -jnp.inf))
                w = cb * lmat * dt_row[h:h + 1, :]
                yh = _dot(w, xh) + y_state[:, hs] * jnp.exp(cum_c)
                c_end = cum[tl - 1:tl, h:h + 1]
                xws.append(xh * (jnp.exp(c_end - cum_c) * dt[:, h:h + 1]))
                ys.append(yh + d_ref[:, h:h + 1] * xh)
            h_add = _dot_tn(jnp.concatenate(xws, axis=1), bm)
            for r in range(SSD_HPG):
                h = g * SSD_HPG + r
                hs = slice(r * SSD_HEADDIM, (r + 1) * SSD_HEADDIM)
                h_scr[bb, h] = jnp.exp(cum[tl - 1:tl, h:h + 1]) * h_grp[hs] + h_add[hs]
        y = jnp.concatenate(ys, axis=1) * _silu(z_ref[bb])
        for g in range(SSD_GROUPS):
            sl = slice(g * gw, (g + 1) * gw)
            y_ref[bb, :, sl] = _rms(y[:, sl], nrm_ref[:, sl])

    @pl.when(i == last)
    def _():
        hout_ref[...] = h_scr[...]


def _ssd(ssd, dtp, lw, h0_all, cbuf_all, layer):
    b, l, _ = ssd.shape
    tl = min(SSD_TILE, l)
    nk = SSD_CONV - 1
    nb = _seqs_per_step(b, l)
    st = pl.BlockSpec((nb, SSD_HEADS, SSD_HEADDIM, SSD_STATE), lambda bi, i: (bi, 0, 0, 0))
    cv = pl.BlockSpec((nb, nk, SSD_CONV_DIM), lambda bi, i: (bi, 0, 0))
    in_specs = [pl.BlockSpec((nb, tl, SSD_CONV_DIM), lambda bi, i: (bi, i, 0)),
                pl.BlockSpec((nb, tl, SSD_INNER), lambda bi, i: (bi, i, SSD_CONV_DIM // SSD_INNER)),
                pl.BlockSpec((nb, tl, V7X_LANES), lambda bi, i: (bi, i, 0)),
                _resident((SSD_CONV, SSD_CONV_DIM)), _resident((1, SSD_CONV_DIM)),
                _resident((1, V7X_LANES)), _resident((1, V7X_LANES)), _resident((1, V7X_LANES)),
                _resident((1, SSD_INNER))]
    args = [ssd, ssd, dtp, lw["ssd_conv_w"], lw["ssd_conv_b"], lw["ssd_dt_bias"], lw["ssd_a_log"],
            lw["ssd_d"], lw["ssd_norm"]]
    if h0_all is not None:
        in_specs += [pl.BlockSpec((None, nb, SSD_HEADS, SSD_HEADDIM, SSD_STATE),
                                  lambda bi, i: (layer, bi, 0, 0, 0)),
                     pl.BlockSpec((None, nb, nk, SSD_CONV_DIM), lambda bi, i: (layer, bi, 0, 0))]
        args += [h0_all, cbuf_all]
    return pl.pallas_call(
        functools.partial(_ssd_kernel, has_init=h0_all is not None),
        grid=(b // nb, l // tl),
        in_specs=in_specs,
        out_specs=[pl.BlockSpec((nb, tl, SSD_INNER), lambda bi, i: (bi, i, 0)), st, cv],
        out_shape=[jax.ShapeDtypeStruct((b, l, SSD_INNER), f32),
                   jax.ShapeDtypeStruct((b, SSD_HEADS, SSD_HEADDIM, SSD_STATE), f32),
                   jax.ShapeDtypeStruct((b, nk, SSD_CONV_DIM), f32)],
        scratch_shapes=[pltpu.VMEM((nb, SSD_HEADS, SSD_HEADDIM, SSD_STATE), f32),
                        pltpu.VMEM((nb, _TAIL_ROWS, SSD_CONV_DIM), f32)],
        compiler_params=_params("parallel", "arbitrary"),
        name="ssd",
    )(*args)


def _sink_softmax(parts, sink):
    m = sink
    for s in parts:
        m = jnp.maximum(m, jnp.max(s, axis=-1, keepdims=True))
    ps = [jnp.exp(s - m) for s in parts]
    den = jnp.exp(sink - m)
    for p in ps:
        den = den + jnp.sum(p, axis=-1, keepdims=True)
    return [p / den for p in ps]


def _stack_rep_heads(q, g):
    h0 = g * SWA_REP
    return jnp.concatenate([q[:, (h0 + r) * SWA_HEAD_DIM:(h0 + r + 1) * SWA_HEAD_DIM]
                            for r in range(SWA_REP)], axis=0)


def _rep_sinks(sink_ref, g, l):
    rep = lax.broadcasted_iota(jnp.int32, (SWA_REP * l, 1), 0) // l
    sink = jnp.zeros((SWA_REP * l, 1), f32)
    for r in range(SWA_REP):
        sink = jnp.where(rep == r, sink_ref[g * SWA_REP + r], sink)
    return sink


def _swa_prompt_kernel(sink_ref, q_ref, kv_ref, kvp_ref, o_ref, *, nwin):
    i = pl.program_id(1)
    w = WINDOW
    rows = SWA_REP * w
    t = lax.broadcasted_iota(jnp.int32, (rows, 2 * w), 0) % w
    j = lax.broadcasted_iota(jnp.int32, (rows, 2 * w), 1)
    rel = t - j + w
    kvp = jnp.where(i > 0, kvp_ref[0], 0.0)
    for win in range(nwin):
        tok = slice(win * w, (win + 1) * w)
        q = q_ref[0, tok, :]
        kv = kv_ref[0, tok, :]
        kpos = (i * nwin + win) * w + j - w
        mask = (rel >= 0) & (rel < w) & (kpos >= 0)
        for g in range(SWA_KV_HEADS):
            ks = slice(g * SWA_HEAD_DIM, (g + 1) * SWA_HEAD_DIM)
            vs = slice(SWA_KW + g * SWA_HEAD_DIM, SWA_KW + (g + 1) * SWA_HEAD_DIM)
            kc = jnp.concatenate([kvp[:, ks], kv[:, ks]], axis=0)
            vc = jnp.concatenate([kvp[:, vs], kv[:, vs]], axis=0)
            s = jnp.where(mask, _dot_nt(_stack_rep_heads(q, g), kc) * (SWA_HEAD_DIM ** -0.5), -jnp.inf)
            (p,) = _sink_softmax([s], _rep_sinks(sink_ref, g, w))
            o = _dot(p, vc)
            for r in range(SWA_REP):
                h = g * SWA_REP + r
                o_ref[0, tok, h * SWA_HEAD_DIM:(h + 1) * SWA_HEAD_DIM] = o[r * w:(r + 1) * w]
        kvp = kv


def _swa_prompt(swa, sinks):
    b, l, _ = swa.shape
    w = WINDOW
    nwin = SWA_WINDOWS_PER_STEP if (l // w) % SWA_WINDOWS_PER_STEP == 0 else 1
    tl = nwin * w
    kvb = SWA_QW // (2 * SWA_KW)
    return pl.pallas_call(
        functools.partial(_swa_prompt_kernel, nwin=nwin),
        grid=(b, l // tl),
        in_specs=[pl.BlockSpec(memory_space=pltpu.SMEM),
                  pl.BlockSpec((1, tl, SWA_QW), lambda bi, i: (bi, i, 0)),
                  pl.BlockSpec((1, tl, 2 * SWA_KW), lambda bi, i: (bi, i, kvb)),
                  pl.BlockSpec((1, w, 2 * SWA_KW), lambda bi, i: (bi, jnp.maximum(i * nwin - 1, 0), kvb))],
        out_specs=pl.BlockSpec((1, tl, SWA_QW), lambda bi, i: (bi, i, 0)),
        out_shape=jax.ShapeDtypeStruct((b, l, SWA_QW), f32),
        compiler_params=_params("parallel", "arbitrary"),
        name="swa_prompt",
    )(sinks, swa, swa, swa)


def _swa_sample_kernel(sink_ref, q_ref, kv_ref, ck_ref, cv_ref, o_ref, ko_ref, vo_ref):
    nb, l = q_ref.shape[0], q_ref.shape[1]
    nbuf = ck_ref.shape[1]
    rows = SWA_REP * l

    def mask(ncols, off):
        t = lax.broadcasted_iota(jnp.int32, (rows, ncols), 0) % l
        j = lax.broadcasted_iota(jnp.int32, (rows, ncols), 1) + off
        rel = t - j + nbuf
        return (rel >= 0) & (rel < WINDOW) & (PAST_LEN - nbuf + j >= 0)

    m_buf = mask(nbuf, 0)
    m_new = mask(l, nbuf)
    scale = SWA_HEAD_DIM ** -0.5
    for bb in range(nb):
        q = q_ref[bb]
        kv = kv_ref[bb]
        for g in range(SWA_KV_HEADS):
            kb = ck_ref[bb, :, g, :]
            vb = cv_ref[bb, :, g, :]
            kn = kv[:, g * SWA_HEAD_DIM:(g + 1) * SWA_HEAD_DIM]
            vn = kv[:, SWA_KW + g * SWA_HEAD_DIM:SWA_KW + (g + 1) * SWA_HEAD_DIM]
            q4 = _stack_rep_heads(q, g)
            s_buf = jnp.where(m_buf, _dot_nt(q4, kb) * scale, -jnp.inf)
            s_new = jnp.where(m_new, _dot_nt(q4, kn) * scale, -jnp.inf)
            p_buf, p_new = _sink_softmax([s_buf, s_new], _rep_sinks(sink_ref, g, l))
            o = _dot(p_buf, vb) + _dot(p_new, vn)
            for r in range(SWA_REP):
                h = g * SWA_REP + r
                o_ref[bb, :, h * SWA_HEAD_DIM:(h + 1) * SWA_HEAD_DIM] = o[r * l:(r + 1) * l]
            ko_ref[bb, 0:nbuf - l, g, :] = kb[l:]
            ko_ref[bb, nbuf - l:, g, :] = kn
            vo_ref[bb, 0:nbuf - l, g, :] = vb[l:]
            vo_ref[bb, nbuf - l:, g, :] = vn


def _swa_sample(swa, ck_all, cv_all, sinks, layer):
    b, l, _ = swa.shape
    nbuf = ck_all.shape[2]
    nb = _seqs_per_step(b, l)
    kvb = SWA_QW // (2 * SWA_KW)
    cache_in = pl.BlockSpec((None, nb, nbuf, SWA_KV_HEADS, SWA_HEAD_DIM), lambda bi: (layer, bi, 0, 0, 0))
    cache_out = pl.BlockSpec((nb, nbuf, SWA_KV_HEADS, SWA_HEAD_DIM), lambda bi: (bi, 0, 0, 0))
    cache_shape = jax.ShapeDtypeStruct((b, nbuf, SWA_KV_HEADS, SWA_HEAD_DIM), f32)
    return pl.pallas_call(
        _swa_sample_kernel,
        grid=(b // nb,),
        in_specs=[pl.BlockSpec(memory_space=pltpu.SMEM),
                  pl.BlockSpec((nb, l, SWA_QW), lambda bi: (bi, 0, 0)),
                  pl.BlockSpec((nb, l, 2 * SWA_KW), lambda bi: (bi, 0, kvb)),
                  cache_in, cache_in],
        out_specs=[pl.BlockSpec((nb, l, SWA_QW), lambda bi: (bi, 0, 0)), cache_out, cache_out],
        out_shape=[jax.ShapeDtypeStruct((b, l, SWA_QW), f32), cache_shape, cache_shape],
        compiler_params=_params("parallel"),
        name="swa_sample",
    )(sinks, swa, swa, ck_all, cv_all)


def _merge_kernel(x_ref, g_ref, og_ref, os_ref, oa_ref, wgate_ref, wbg_ref, wbs_ref, wba_ref, wout_ref, y_ref):
    x = x_ref[...]
    d = x.shape[1]
    xb = _rms(x, g_ref[...]).astype(bf16)
    merged = None
    for j, (br_ref, w_ref) in enumerate(((og_ref, wbg_ref), (os_ref, wbs_ref), (oa_ref, wba_ref))):
        gate = _sigmoid(jnp.dot(xb, wgate_ref[:, j * d:(j + 1) * d], preferred_element_type=f32))
        term = gate * _dot(br_ref[...], w_ref[...])
        merged = term if merged is None else merged + term
    y_ref[...] = x + _dot(merged, wout_ref[...])


def _merge(x2, g, o_gla, y_ssd, o_swa, lw):
    t, d = x2.shape
    tm = min(TOKEN_TILE, t)
    row = lambda w: pl.BlockSpec((tm, w), lambda i: (i, 0))
    ws = [lw["w_gate"], lw["w_br_gla"], lw["w_br_ssd"], lw["w_br_swa"], lw["w_out"]]
    return pl.pallas_call(
        _merge_kernel,
        grid=(t // tm,),
        in_specs=[row(d), _resident((1, d)), row(GLA_W), row(SSD_INNER), row(SWA_QW)]
                 + [_resident(w.shape) for w in ws],
        out_specs=row(d),
        out_shape=jax.ShapeDtypeStruct((t, d), f32),
        compiler_params=_params("parallel"),
        name="merge",
    )(x2, g, o_gla, y_ssd, o_swa, *ws)


def _rms_matmul_kernel(x_ref, g_ref, w_ref, y_ref):
    y_ref[...] = jnp.dot(_rms(x_ref[...], g_ref[...]).astype(bf16), w_ref[...], preferred_element_type=f32)


def _rms_matmul(x2, g, w):
    t, d = x2.shape
    n = w.shape[1]
    tm = min(TOKEN_TILE, t)
    return pl.pallas_call(
        _rms_matmul_kernel,
        grid=(t // tm,),
        in_specs=[pl.BlockSpec((tm, d), lambda i: (i, 0)), _resident((1, d)), _resident(w.shape)],
        out_specs=pl.BlockSpec((tm, n), lambda i: (i, 0)),
        out_shape=jax.ShapeDtypeStruct((t, n), f32),
        compiler_params=_params("parallel"),
        name="rms_matmul",
    )(x2, g, w)


def _matmul_res_kernel(a_ref, w_ref, x_ref, y_ref):
    y_ref[...] = x_ref[...] + _dot(a_ref[...], w_ref[...])


def _matmul_res(a2, w, x2):
    t, d = x2.shape
    tm = min(TOKEN_TILE, t)
    return pl.pallas_call(
        _matmul_res_kernel,
        grid=(t // tm,),
        in_specs=[pl.BlockSpec((tm, a2.shape[1]), lambda i: (i, 0)), _resident(w.shape),
                  pl.BlockSpec((tm, d), lambda i: (i, 0))],
        out_specs=pl.BlockSpec((tm, d), lambda i: (i, 0)),
        out_shape=jax.ShapeDtypeStruct((t, d), f32),
        compiler_params=_params("parallel"),
        name="matmul_res",
    )(a2, w, x2)


def _mem_fused_kernel(x_ref, g_ref, wq_ref, k_ref, v_ref, wo_ref, y_ref):
    x = x_ref[...]
    q = jnp.dot(_rms(x, g_ref[...]).astype(bf16), wq_ref[...], preferred_element_type=f32)
    hd = q.shape[1] // MEM_HEADS
    os = []
    for h in range(MEM_HEADS):
        sl = slice(h * hd, (h + 1) * hd)
        p = _softmax_rows(_dot_nt(q[:, sl], k_ref[0, :, sl]) * (hd ** -0.5))
        os.append(_dot(p, v_ref[0, :, sl]))
    y_ref[...] = x + _dot(jnp.concatenate(os, axis=1), wo_ref[...])


def _mem_fused(x2, b, g, wq, wo, mk3, mv3):
    t, d = x2.shape
    l = t // b
    m = mk3.shape[1]
    tq = min(TOKEN_TILE, l)
    nq = l // tq
    row = pl.BlockSpec((tq, d), lambda bi, i: (bi * nq + i, 0))
    kv = pl.BlockSpec((1, m, d), lambda bi, i: (bi, 0, 0))
    return pl.pallas_call(
        _mem_fused_kernel,
        grid=(b, nq),
        in_specs=[row, _resident((1, d)), _resident(wq.shape), kv, kv, _resident(wo.shape)],
        out_specs=row,
        out_shape=jax.ShapeDtypeStruct((t, d), f32),
        compiler_params=_params("parallel", "arbitrary"),
        name="mem_fused",
    )(x2, g, wq, mk3, mv3, wo)


def _mem_cache_attn_kernel(q_ref, k_ref, v_ref, o_ref, kh_scr, vh_scr):
    nb = q_ref.shape[0]
    hd = k_ref.shape[3]
    for bb in range(nb):
        for h in range(MEM_HEADS):
            sl = slice(h * hd, (h + 1) * hd)
            kh_scr[...] = k_ref[bb, :, h, :]
            vh_scr[...] = v_ref[bb, :, h, :]
            p = _softmax_rows(_dot_nt(q_ref[bb, :, sl], kh_scr[...]) * (hd ** -0.5))
            o_ref[bb, :, sl] = _dot(p, vh_scr[...])


def _mem_cache_attn(q3, ck_all, cv_all, layer):
    b, l, d = q3.shape
    _, _, m, nh, hd = ck_all.shape
    nb = MEM_CACHE_BATCH if b % MEM_CACHE_BATCH == 0 else 1
    cache = pl.BlockSpec((None, nb, m, nh, hd), lambda bi: (layer, bi, 0, 0, 0))
    return pl.pallas_call(
        _mem_cache_attn_kernel,
        grid=(b // nb,),
        in_specs=[pl.BlockSpec((nb, l, d), lambda bi: (bi, 0, 0)), cache, cache],
        out_specs=pl.BlockSpec((nb, l, d), lambda bi: (bi, 0, 0)),
        out_shape=jax.ShapeDtypeStruct((b, l, d), f32),
        scratch_shapes=[pltpu.VMEM((m, hd), f32), pltpu.VMEM((m, hd), f32)],
        compiler_params=_params("parallel"),
        name="mem_cache_attn",
    )(q3, ck_all, cv_all)


def _moe_kernel(x_ref, g_ref, wrh_ref, wrl_ref, br_ref, wg_ref, wu_ref, wd_ref, gf_ref, o_ref,
                xb_scr, comb_scr, *, final_norm):
    grp = pl.program_id(1)
    tm = x_ref.shape[0]
    lane = lax.broadcasted_iota(jnp.int32, (tm, V7X_LANES), 1)
    lane_f = lane.astype(f32)
    first = lambda hit: jnp.min(jnp.where(hit, lane_f, float(V7X_LANES)), axis=-1, keepdims=True)

    @pl.when(grp == 0)
    def _():
        xn = _rms(x_ref[...], g_ref[...])
        xh = xn.astype(bf16)
        xl = (xn - xh.astype(f32)).astype(bf16)
        xb_scr[...] = xh
        d = lambda a, w: jnp.dot(a, w[...], preferred_element_type=f32)
        logits = d(xh, wrh_ref) + (d(xl, wrh_ref) + d(xh, wrl_ref)) + br_ref[...]
        lg = jnp.where(lane < N_GROUPS, logits[:, :V7X_LANES], -jnp.inf)
        mg = jnp.max(lg, axis=-1, keepdims=True)
        gi = first(lg == mg)
        pg = 1.0 / jnp.sum(jnp.exp(lg - mg), axis=-1, keepdims=True)
        in_grp = ((lane // EXPERTS_PER_GROUP).astype(f32) == gi) & (lane < N_EXPERTS)
        le = jnp.where(in_grp, logits[:, V7X_LANES:], -jnp.inf)
        m1 = jnp.max(le, axis=-1, keepdims=True)
        i1 = first(le == m1)
        z = jnp.sum(jnp.exp(le - m1), axis=-1, keepdims=True)
        le2 = jnp.where(lane_f == i1, -jnp.inf, le)
        m2 = jnp.max(le2, axis=-1, keepdims=True)
        i2 = first(le2 == m2)
        v1 = 1.0 / z
        v2 = jnp.exp(m2 - m1) / z
        tot = v1 + v2
        comb_scr[...] = jnp.where(lane_f == i1, pg * v1 / tot, jnp.where(lane_f == i2, pg * v2 / tot, 0.0))
        o_ref[...] = jnp.zeros(o_ref.shape, f32)

    xb = xb_scr[...]
    comb = comb_scr[...]
    acc = jnp.zeros(o_ref.shape, f32)
    for e in range(EXPERTS_PER_GROUP):
        ce = jnp.sum(jnp.where(lane == grp * EXPERTS_PER_GROUP + e, comb, 0.0), axis=-1, keepdims=True)
        hg = jnp.dot(xb, wg_ref[e], preferred_element_type=f32)
        hu = jnp.dot(xb, wu_ref[e], preferred_element_type=f32)
        acc = acc + _dot(_silu(hg) * hu * ce, wd_ref[e])
    o_ref[...] += acc

    @pl.when(grp == pl.num_programs(1) - 1)
    def _():
        y = x_ref[...] + o_ref[...]
        o_ref[...] = _rms(y, gf_ref[...]) if final_norm else y


def _moe(x2, g, lw, g_final, final_norm):
    t, d = x2.shape
    tm = min(MOE_TILE, t)
    ff = lw["moe_w_gate"].shape[2]
    e = EXPERTS_PER_GROUP
    return pl.pallas_call(
        functools.partial(_moe_kernel, final_norm=final_norm),
        grid=(t // tm, N_GROUPS),
        in_specs=[pl.BlockSpec((tm, d), lambda i, j: (i, 0)), _resident((1, d)),
                  _resident(lw["moe_wr_hi"].shape), _resident(lw["moe_wr_lo"].shape),
                  _resident((1, 2 * V7X_LANES)),
                  pl.BlockSpec((e, d, ff), lambda i, j: (j, 0, 0)),
                  pl.BlockSpec((e, d, ff), lambda i, j: (j, 0, 0)),
                  pl.BlockSpec((e, ff, d), lambda i, j: (j, 0, 0)),
                  _resident((1, d))],
        out_specs=pl.BlockSpec((tm, d), lambda i, j: (i, 0)),
        out_shape=jax.ShapeDtypeStruct((t, d), f32),
        scratch_shapes=[pltpu.VMEM((tm, d), bf16), pltpu.VMEM((tm, V7X_LANES), f32)],
        compiler_params=_params("parallel", "arbitrary"),
        name="moe",
    )(x2, g, lw["moe_wr_hi"], lw["moe_wr_lo"], lw["moe_br"], lw["moe_w_gate"], lw["moe_w_up"],
      lw["moe_w_down"], g_final)


def _pad_cols(w, n):
    return jnp.pad(w, ((0, 0), (0, n - w.shape[1])))


def _pad_lanes(v):
    return _pad_cols(v.reshape(1, -1), V7X_LANES)


def _layer_weights(p, l):
    w_in = p["w_in"][l]
    sizes = (GLA_W, GLA_W, GLA_W, GLA_W, GLA_LOWRANK, SSD_INNER, SSD_CONV_DIM, SSD_HEADS,
             SWA_QW, SWA_KW, SWA_KW)
    offs, o = [], 0
    for n in sizes:
        offs.append(o)
        o += n
    seg = lambda idx: w_in[:, offs[idx]:offs[idx] + sizes[idx]]
    gq, gk, gv, gr, ga, sz, sxbc, sdt, aq, ak, av = (seg(i) for i in range(len(sizes)))
    w_pack = jnp.concatenate([gq, gk, gv, gr, _pad_cols(ga, V7X_LANES), sxbc, sz,
                              _pad_cols(sdt, V7X_LANES), aq, ak, av], axis=1).astype(bf16)
    d = w_in.shape[0]
    router = jnp.concatenate([_pad_cols(p["moe_w_group"][l], V7X_LANES),
                              _pad_cols(p["moe_w_expert"][l], V7X_LANES)], axis=1)
    router_hi = router.astype(bf16)
    return dict(
        w_pack=w_pack,
        w_gate=w_in[:, o:o + N_BRANCHES * d].astype(bf16),
        wa2=jnp.pad(p["gla_w_a2"][l], ((0, V7X_LANES - GLA_LOWRANK), (0, 0))).astype(bf16),
        gla_b_a=p["gla_b_a"][l].reshape(1, -1),
        gla_norm=p["gla_norm"][l].reshape(1, -1),
        ssd_conv_w=p["ssd_conv_w"][l],
        ssd_conv_b=p["ssd_conv_b"][l].reshape(1, -1),
        ssd_dt_bias=_pad_lanes(p["ssd_dt_bias"][l]),
        ssd_a_log=_pad_lanes(p["ssd_a_log"][l]),
        ssd_d=_pad_lanes(p["ssd_d"][l]),
        ssd_norm=p["ssd_norm"][l].reshape(1, -1),
        swa_sinks=p["swa_sinks"][l],
        w_br_gla=p["w_br_gla"][l].astype(bf16),
        w_br_ssd=p["w_br_ssd"][l].astype(bf16),
        w_br_swa=p["w_br_swa"][l].astype(bf16),
        w_out=p["w_out"][l].astype(bf16),
        norm_mix=p["norm_mix"][l].reshape(1, -1),
        norm_mem=p["norm_mem"][l].reshape(1, -1),
        norm_memkv=p["norm_memkv"][l].reshape(1, -1),
        norm_ffn=p["norm_ffn"][l].reshape(1, -1),
        mem_wq=p["mem_wq"][l].astype(bf16),
        mem_wk=p["mem_wk"][l].astype(bf16),
        mem_wv=p["mem_wv"][l].astype(bf16),
        mem_wo=p["mem_wo"][l].astype(bf16),
        moe_wr_hi=router_hi,
        moe_wr_lo=(router - router_hi.astype(f32)).astype(bf16),
        moe_br=jnp.concatenate([_pad_lanes(p["moe_b_group"][l]), _pad_lanes(p["moe_b_expert"][l])], axis=1),
        moe_w_gate=p["moe_w_gate"][l].astype(bf16),
        moe_w_up=p["moe_w_up"][l].astype(bf16),
        moe_w_down=p["moe_w_down"][l].astype(bf16),
    )


def _mixer(h3, lw, layer, state):
    b, l, d = h3.shape
    x2 = h3.reshape(b * l, d)
    gla_s0, ssd_h0, conv_buf, swa_kbuf, swa_vbuf = state if state is not None else (None,) * 5
    gla5, ssd, dtp, swa = _in_proj(x2, lw["norm_mix"], lw["w_pack"], lw["wa2"], lw["gla_b_a"])
    o_gla, gla_s = _gla(gla5.reshape(b, l, GLA5_W), lw["gla_norm"], gla_s0, layer)
    y_ssd, ssd_h, conv_s = _ssd(ssd.reshape(b, l, SSD_W), dtp.reshape(b, l, V7X_LANES), lw, ssd_h0, conv_buf,
                                layer)
    swa3 = swa.reshape(b, l, SWA_W)
    if state is None:
        o_swa = _swa_prompt(swa3, lw["swa_sinks"])
        nkeep = min(WINDOW, l)
        kv_shape = (b, nkeep, SWA_KV_HEADS, SWA_HEAD_DIM)
        swa_k = swa3[:, l - nkeep:, SWA_QW:SWA_QW + SWA_KW].reshape(kv_shape)
        swa_v = swa3[:, l - nkeep:, SWA_QW + SWA_KW:].reshape(kv_shape)
    else:
        o_swa, swa_k, swa_v = _swa_sample(swa3, swa_kbuf, swa_vbuf, lw["swa_sinks"], layer)
    y2 = _merge(x2, lw["norm_mix"], o_gla.reshape(b * l, GLA_W), y_ssd.reshape(b * l, SSD_INNER),
                o_swa.reshape(b * l, SWA_QW), lw)
    return y2, gla_s, ssd_h, conv_s, swa_k, swa_v


def kernel(x_prompt, x_sample, mem_prompt, cache_mem_k, cache_mem_v, cache_swa_k, cache_swa_v, state_gla, state_ssd, state_conv, norm_mix, w_in, gla_w_a2, gla_b_a, gla_norm, ssd_conv_w, ssd_conv_b, ssd_dt_bias, ssd_a_log, ssd_d, ssd_norm, swa_sinks, w_br_gla, w_br_ssd, w_br_swa, w_out, norm_mem, norm_memkv, mem_wq, mem_wk, mem_wv, mem_wo, norm_ffn, moe_w_group, moe_b_group, moe_w_expert, moe_b_expert, moe_w_gate, moe_w_up, moe_w_down, norm_final):
    p = dict(norm_mix=norm_mix, w_in=w_in, gla_w_a2=gla_w_a2, gla_b_a=gla_b_a, gla_norm=gla_norm,
             ssd_conv_w=ssd_conv_w, ssd_conv_b=ssd_conv_b, ssd_dt_bias=ssd_dt_bias, ssd_a_log=ssd_a_log,
             ssd_d=ssd_d, ssd_norm=ssd_norm, swa_sinks=swa_sinks, w_br_gla=w_br_gla, w_br_ssd=w_br_ssd,
             w_br_swa=w_br_swa, w_out=w_out, norm_mem=norm_mem, norm_memkv=norm_memkv, mem_wq=mem_wq,
             mem_wk=mem_wk, mem_wv=mem_wv, mem_wo=mem_wo, norm_ffn=norm_ffn, moe_w_group=moe_w_group,
             moe_b_group=moe_b_group, moe_w_expert=moe_w_expert, moe_b_expert=moe_b_expert,
             moe_w_gate=moe_w_gate, moe_w_up=moe_w_up, moe_w_down=moe_w_down)
    depth = w_in.shape[0]
    bp, lp, d = x_prompt.shape
    bs, ls, _ = x_sample.shape
    mlen = mem_prompt.shape[1]
    g_final = norm_final.reshape(1, -1)
    mem2 = mem_prompt.reshape(bp * mlen, d)
    sample_state = (state_gla, state_ssd, state_conv, cache_swa_k, cache_swa_v)
    hp = x_prompt
    hs = x_sample
    outs = [[] for _ in range(12)]
    for l in range(depth):
        lw = _layer_weights(p, l)
        last = l == depth - 1
        hp2, g_s, h_s, c_s, k_s, v_s = _mixer(hp, lw, l, None)
        mk = _rms_matmul(mem2, lw["norm_memkv"], lw["mem_wk"])
        mv = _rms_matmul(mem2, lw["norm_memkv"], lw["mem_wv"])
        hp2 = _mem_fused(hp2, bp, lw["norm_mem"], lw["mem_wq"], lw["mem_wo"],
                         mk.reshape(bp, mlen, d), mv.reshape(bp, mlen, d))
        hp = _moe(hp2, lw["norm_ffn"], lw, g_final, last).reshape(bp, lp, d)
        mem_shape = (bp, mlen, MEM_HEADS, d // MEM_HEADS)
        for lst, val in zip(outs[:7], (mk.reshape(mem_shape), mv.reshape(mem_shape), k_s, v_s, g_s, h_s, c_s)):
            lst.append(val)
        hs2, g_s, h_s, c_s, k_s, v_s = _mixer(hs, lw, l, sample_state)
        q = _rms_matmul(hs2, lw["norm_mem"], lw["mem_wq"])
        o = _mem_cache_attn(q.reshape(bs, ls, d), cache_mem_k, cache_mem_v, l)
        hs2 = _matmul_res(o.reshape(bs * ls, d), lw["mem_wo"], hs2)
        hs = _moe(hs2, lw["norm_ffn"], lw, g_final, last).reshape(bs, ls, d)
        for lst, val in zip(outs[7:], (k_s, v_s, g_s, h_s, c_s)):
            lst.append(val)
    return (hp, hs) + tuple(jnp.stack(o) for o in outs)
```
